```python
import jax, jax.numpy as jnp
from jax import lax
import numpy as np

D_MODEL = 1024
BATCH = 8
SEQ = 8192
DEPTH = 2

MIX_WIDTH = D_MODEL
N_MIXERS = 4
GROUP_WIDTH = MIX_WIDTH // N_MIXERS
HEAD_DIM = 64
HEADS_PER_GROUP = GROUP_WIDTH // HEAD_DIM
CONV_WIDTH = 3
POOL_WINDOWS = (2, 4, 8, 16)
POOL_CH = GROUP_WIDTH // len(POOL_WINDOWS)
CHUNK = 128
D_FF = -(-8 * D_MODEL // (3 * 256)) * 256
N_PROJ_SLICES = 7
PROJ_WIDTH = N_PROJ_SLICES * GROUP_WIDTH
EPS = 1e-6

kernel_name = "hybrid_parallel_mixer_encoder"


def rmsnorm(x, g):
    xf = x.astype(jnp.float32)
    y = xf * lax.rsqrt(jnp.mean(xf * xf, axis=-1, keepdims=True) + EPS)
    return (y * g.astype(jnp.float32)).astype(x.dtype)


def layernorm_noaffine(x):
    xf = x.astype(jnp.float32)
    mu = jnp.mean(xf, axis=-1, keepdims=True)
    xc = xf - mu
    y = xc * lax.rsqrt(jnp.mean(xc * xc, axis=-1, keepdims=True) + EPS)
    return y.astype(x.dtype)


def short_conv(z, w):
    zp = jnp.pad(z, ((0, 0), (1, 1), (0, 0)))
    return zp[:, :-2] * w[0] + zp[:, 1:-1] * w[1] + zp[:, 2:] * w[2]


def multiscale_pool(p, pool_w, pool_scale):
    B, S, _ = p.shape
    pf = p.astype(jnp.float32)
    cs = jnp.concatenate([jnp.zeros_like(pf[:, :1]), jnp.cumsum(pf, axis=1)], axis=1)
    t = jnp.arange(S)
    outs = []
    for g, w in enumerate(POOL_WINDOWS):
        lo = jnp.clip(t - w // 2, 0, S)
        hi = jnp.clip(t + w // 2, 0, S)
        sl = slice(g * POOL_CH, (g + 1) * POOL_CH)
        csg = cs[..., sl]
        cnt = (hi - lo).astype(jnp.float32)[None, :, None]
        mean = (jnp.take(csg, hi, axis=1) - jnp.take(csg, lo, axis=1)) / cnt
        outs.append(mean - pf[..., sl])
    d = jnp.stack(outs, axis=2).astype(p.dtype)
    y = jnp.einsum("bsgc,gcd->bsgd", d, pool_w).reshape(B, S, GROUP_WIDTH)
    return y * pool_scale


def fourier_mix(f, fourier_w):
    B, S, _ = f.shape
    fh = f.astype(jnp.float32).reshape(B, S, HEADS_PER_GROUP, HEAD_DIM)
    spec = jnp.fft.fft2(fh, axes=(1, 3), norm="ortho").real.astype(f.dtype)
    return jnp.einsum("bshc,hcd->bshd", spec, fourier_w).reshape(B, S, GROUP_WIDTH)


def spatial_gate(u, v, spatial_w, spatial_b):
    B, S, _ = u.shape
    n = S // CHUNK
    vh = layernorm_noaffine(v.reshape(B, S, HEADS_PER_GROUP, HEAD_DIM))
    vh = vh.reshape(B, n, CHUNK, HEADS_PER_GROUP, HEAD_DIM)
    s = jnp.einsum("hpq,bnqhc->bnphc", spatial_w, vh) + spatial_b.T[None, None, :, :, None]
    return u * s.reshape(B, S, GROUP_WIDTH)


def hybrid_mixer(h, w_in, conv_w, pool_w, pool_scale, fourier_w, spatial_w, spatial_b,
                 group_norm_gain, w_out):
    B, S, _ = h.shape
    proj = h @ w_in
    b_gate, c_gate, z, p, f, u, v = jnp.split(proj, N_PROJ_SLICES, axis=-1)
    y_conv = b_gate * short_conv(c_gate * z, conv_w)
    y_pool = multiscale_pool(p, pool_w, pool_scale)
    y_four = fourier_mix(f, fourier_w)
    y_gmlp = spatial_gate(u, v, spatial_w, spatial_b)
    y = jnp.stack([y_conv, y_pool, y_four, y_gmlp], axis=2)
    y = rmsnorm(y, group_norm_gain.reshape(N_MIXERS, GROUP_WIDTH)).reshape(B, S, MIX_WIDTH)
    return y @ w_out


def swiglu(h, w_gate, w_up, w_down):
    return (jax.nn.silu(h @ w_gate) * (h @ w_up)) @ w_down


def setup_inputs(seed: int = 0) -> dict:
    key = jax.random.key(seed)
    ks = jax.random.split(key, 17)

    def nrm(k, shape, scale):
        return jax.random.normal(k, shape, jnp.float32) * scale

    def gain(k, shape):
        return 1.0 + nrm(k, shape, 0.05)

    return {
        "x": nrm(ks[0], (BATCH, SEQ, D_MODEL), 1.0),
        "pre_mix_gain": gain(ks[1], (DEPTH, D_MODEL)),
        "post_mix_gain": gain(ks[2], (DEPTH, D_MODEL)),
        "pre_ffn_gain": gain(ks[3], (DEPTH, D_MODEL)),
        "post_ffn_gain": gain(ks[4], (DEPTH, D_MODEL)),
        "w_in": nrm(ks[5], (DEPTH, D_MODEL, PROJ_WIDTH), D_MODEL ** -0.5),
        "conv_w": nrm(ks[6], (DEPTH, CONV_WIDTH, GROUP_WIDTH), CONV_WIDTH ** -0.5),
        "pool_w": nrm(ks[7], (DEPTH, len(POOL_WINDOWS), POOL_CH, POOL_CH), POOL_CH ** -0.5),
        "pool_scale": gain(ks[8], (DEPTH, GROUP_WIDTH)),
        "fourier_w": nrm(ks[9], (DEPTH, HEADS_PER_GROUP, HEAD_DIM, HEAD_DIM), HEAD_DIM ** -0.5),
        "spatial_w": nrm(ks[10], (DEPTH, HEADS_PER_GROUP, CHUNK, CHUNK), CHUNK ** -0.5),
        "spatial_b": 1.0 + nrm(ks[11], (DEPTH, HEADS_PER_GROUP, CHUNK), 0.02),
        "group_norm_gain": gain(ks[12], (DEPTH, MIX_WIDTH)),
        "w_out": nrm(ks[13], (DEPTH, MIX_WIDTH, D_MODEL), MIX_WIDTH ** -0.5),
        "w_gate": nrm(ks[14], (DEPTH, D_MODEL, D_FF), D_MODEL ** -0.5),
        "w_up": nrm(ks[15], (DEPTH, D_MODEL, D_FF), D_MODEL ** -0.5),
        "w_down": nrm(ks[16], (DEPTH, D_FF, D_MODEL), D_FF ** -0.5),
    }


def reference(x, pre_mix_gain, post_mix_gain, pre_ffn_gain, post_ffn_gain, w_in, conv_w,
              pool_w, pool_scale, fourier_w, spatial_w, spatial_b, group_norm_gain, w_out,
              w_gate, w_up, w_down):
    for l in range(DEPTH):
        h = rmsnorm(x, pre_mix_gain[l])
        m = hybrid_mixer(h, w_in[l], conv_w[l], pool_w[l], pool_scale[l], fourier_w[l],
                         spatial_w[l], spatial_b[l], group_norm_gain[l], w_out[l])
        x = x + rmsnorm(m, post_mix_gain[l])
        h = rmsnorm(x, pre_ffn_gain[l])
        f = swiglu(h, w_gate[l], w_up[l], w_down[l])
        x = x + rmsnorm(f, post_ffn_gain[l])
    return x
```

```python
import functools

import jax
import jax.numpy as jnp
import numpy as np
from jax import lax
from jax.experimental import pallas as pl
from jax.experimental.pallas import tpu as pltpu

D_MODEL = 1024
GROUP = 256
HEAD_DIM = 64
N_HEADS = GROUP // HEAD_DIM
CHUNK = 128
D_FF = 2816
POOL_WINDOWS = (2, 4, 8, 16)
EPS = 1e-6

V7X_LANES = 128
V7X_SUBLANES = 8
V7X_MXU_COLS = 256
V7X_VMEM_LIMIT_BYTES = 56 * 1024 * 1024

SEQ_LO = 128
HALO = 16
TOKEN_TILE = 512
FF_CHUNK = V7X_MXU_COLS

F32 = jnp.float32
BF16 = jnp.bfloat16


def _dot(a, b):
    return jnp.dot(a, b, preferred_element_type=F32)


def _rms(x, g):
    ms = jnp.mean(x * x, axis=-1, keepdims=True)
    return x * lax.rsqrt(ms + EPS) * g


def _fourier_tables(seq):
    seq_hi = seq // SEQ_LO
    cm = np.outer(np.arange(HEAD_DIM), np.arange(HEAD_DIM)) % HEAD_DIM
    ang = 2.0 * np.pi * cm / HEAD_DIM
    fc = np.zeros((GROUP, 2 * GROUP))
    for h in range(N_HEADS):
        sl = slice(h * HEAD_DIM, (h + 1) * HEAD_DIM)
        fc[sl, sl] = np.cos(ang) / np.sqrt(HEAD_DIM)
        fc[sl, GROUP + h * HEAD_DIM:GROUP + (h + 1) * HEAD_DIM] = -np.sin(ang) / np.sqrt(HEAD_DIM)
    kt = np.outer(np.arange(seq_hi), np.arange(seq_hi)) % seq_hi
    ang1 = 2.0 * np.pi * kt / seq_hi
    eye = np.eye(V7X_SUBLANES)
    c1 = np.kron(np.cos(ang1) / np.sqrt(seq_hi), eye)
    s1 = np.kron(np.sin(ang1) / np.sqrt(seq_hi), eye)
    kd = np.block([[c1, s1], [-s1, c1]])
    t_lo = np.arange(SEQ_LO).reshape(SEQ_LO // V7X_SUBLANES, 1, V7X_SUBLANES)
    k_lo = np.arange(seq_hi).reshape(1, seq_hi, 1)
    angt = 2.0 * np.pi * ((t_lo * k_lo) % seq) / seq
    angt = angt.reshape(SEQ_LO // V7X_SUBLANES, seq_hi * V7X_SUBLANES, 1)
    twc = np.broadcast_to(np.cos(angt), angt.shape[:2] + (V7X_LANES,))
    tws = np.broadcast_to(np.sin(angt), angt.shape[:2] + (V7X_LANES,))
    k2 = np.outer(np.arange(SEQ_LO), np.arange(SEQ_LO)) % SEQ_LO
    ang2 = 2.0 * np.pi * k2 / SEQ_LO
    e = np.concatenate([np.cos(ang2), np.sin(ang2)], axis=1) / np.sqrt(SEQ_LO)
    return (jnp.asarray(fc, F32).astype(BF16), jnp.asarray(kd, F32).astype(BF16), jnp.asarray(twc, F32),
            jnp.asarray(tws, F32), jnp.asarray(e, F32).astype(BF16))


def _fourier_kernel(x_ref, g_ref, wf_ref, fc_ref, kd_ref, twc_ref, tws_ref, e_ref, o_ref, y_s,
                    *, seq_hi, out_chunk):
    tb = pl.program_id(1)
    rows = seq_hi * V7X_SUBLANES
    x = x_ref[0].reshape(rows, D_MODEL)
    h = _rms(x, g_ref[...]).astype(BF16)
    f = _dot(h, wf_ref[...])
    g = _dot(f.astype(BF16), fc_ref[...])
    gs = jnp.concatenate([g[:, :GROUP], g[:, GROUP:]], axis=0).astype(BF16)
    y = _dot(kd_ref[...], gs)
    yr, yi = y[:rows], y[rows:]
    c = jnp.concatenate([twc_ref[0]] * (GROUP // V7X_LANES), axis=1)
    s = jnp.concatenate([tws_ref[0]] * (GROUP // V7X_LANES), axis=1)
    yr2 = yr * c + yi * s
    yi2 = yi * c - yr * s
    row = pl.multiple_of(tb * V7X_SUBLANES, V7X_SUBLANES)
    for k in range(seq_hi):
        rs = slice(k * V7X_SUBLANES, (k + 1) * V7X_SUBLANES)
        ls = slice(k * GROUP, (k + 1) * GROUP)
        y_s[pl.ds(row, V7X_SUBLANES), ls] = yr2[rs]
        y_s[pl.ds(SEQ_LO + row, V7X_SUBLANES), ls] = yi2[rs]

    @pl.when(tb == pl.num_programs(1) - 1)
    def _():
        for k in range(0, seq_hi * GROUP, out_chunk):
            ls = slice(k, k + out_chunk)
            o_ref[0, :, ls] = _dot(e_ref[...], y_s[:, ls].astype(BF16)).astype(o_ref.dtype)


def _const_spec(shape):
    n = len(shape)
    return pl.BlockSpec(shape, lambda *_: (0,) * n, pipeline_mode=pl.Buffered(1))


def _fourier_spec(x, gain, wf, tables):
    b, seq, _ = x.shape
    seq_hi = seq // SEQ_LO
    n_tb = SEQ_LO // V7X_SUBLANES
    rows = seq_hi * V7X_SUBLANES
    fc, kd, twc, tws, e = tables
    out_chunk = min(seq_hi * GROUP, 8 * GROUP)
    x4 = x.reshape(b, seq_hi, SEQ_LO, D_MODEL)
    out = pl.pallas_call(
        functools.partial(_fourier_kernel, seq_hi=seq_hi, out_chunk=out_chunk),
        grid=(b, n_tb),
        in_specs=[
            pl.BlockSpec((1, seq_hi, V7X_SUBLANES, D_MODEL), lambda i, t: (i, 0, t, 0)),
            _const_spec((1, D_MODEL)),
            _const_spec((D_MODEL, GROUP)),
            _const_spec((GROUP, 2 * GROUP)),
            _const_spec((2 * rows, 2 * rows)),
            pl.BlockSpec((1, rows, V7X_LANES), lambda i, t: (t, 0, 0)),
            pl.BlockSpec((1, rows, V7X_LANES), lambda i, t: (t, 0, 0)),
            _const_spec((SEQ_LO, 2 * SEQ_LO)),
        ],
        out_specs=pl.BlockSpec((1, SEQ_LO, seq_hi * GROUP), lambda i, t: (i, 0, 0)),
        out_shape=jax.ShapeDtypeStruct((b, SEQ_LO, seq_hi * GROUP), BF16),
        scratch_shapes=[pltpu.VMEM((2 * SEQ_LO, seq_hi * GROUP), F32)],
        compiler_params=pltpu.CompilerParams(
            dimension_semantics=("arbitrary", "arbitrary"),
            vmem_limit_bytes=V7X_VMEM_LIMIT_BYTES),
        name="fourier_spec",
    )(x4, gain, wf, fc, kd, twc, tws, e)
    return out.reshape(b * seq, GROUP)


def _mixer_ffn_kernel(x_ref, xp_ref, xn_ref, spec_ref, g_pre_ref, g_post_ref, g_pre_ffn_ref,
                      g_post_ffn_ref, g_grp_ref, w_in_ref, conv_w_ref, pool_w_ref, pool_scale_ref,
                      four_w_ref, mavg_ref, wcat_ref, sbias_ref, w_out_ref, wg_ref, wu_ref, wd_ref,
                      o_ref, cz_s, p_s, act_s, *, tile, seq):
    tiles_per_seq = seq // tile
    ti = pl.program_id(0) % tiles_per_seq
    x = x_ref[...]
    xh = jnp.concatenate([xp_ref[...], xn_ref[...]], axis=0)

    hm = _rms(x, g_pre_ref[...]).astype(BF16)
    hh = _rms(xh, g_pre_ref[...]).astype(BF16)
    pm = _dot(hm, w_in_ref[...])
    ph = _dot(hh, w_in_ref[:, :3 * GROUP])
    hrow = lax.broadcasted_iota(jnp.int32, (2 * HALO, 1), 0)
    has_prev = (ti != 0).astype(jnp.int32)
    has_next = (ti != tiles_per_seq - 1).astype(jnp.int32)
    ph = jnp.where(jnp.where(hrow < HALO, has_prev, has_next) != 0, ph, 0.0)

    cz_s[pl.ds(0, HALO), :] = ph[:HALO, :GROUP] * ph[:HALO, GROUP:2 * GROUP]
    cz_s[pl.ds(HALO, tile), :] = pm[:, :GROUP] * pm[:, GROUP:2 * GROUP]
    cz_s[pl.ds(HALO + tile, HALO), :] = ph[HALO:, :GROUP] * ph[HALO:, GROUP:2 * GROUP]
    p_s[pl.ds(0, HALO), :] = ph[:HALO, 2 * GROUP:]
    p_s[pl.ds(HALO, tile), :] = pm[:, 2 * GROUP:3 * GROUP]
    p_s[pl.ds(HALO + tile, HALO), :] = ph[HALO:, 2 * GROUP:]
    b_gate = pm[:, 3 * GROUP:4 * GROUP]
    u = pm[:, 4 * GROUP:5 * GROUP]
    v = pm[:, 5 * GROUP:]

    y_conv = b_gate * (conv_w_ref[0:1, :] * cz_s[pl.ds(HALO - 1, tile), :]
                       + conv_w_ref[1:2, :] * cz_s[pl.ds(HALO, tile), :]
                       + conv_w_ref[2:3, :] * cz_s[pl.ds(HALO + 1, tile), :])

    pos = ti * tile + lax.broadcasted_iota(jnp.int32, (tile, 1), 0)
    lane = lax.broadcasted_iota(jnp.int32, (1, V7X_LANES), 1)
    first_half = lane < HEAD_DIM
    d_cols = []
    for col in range(GROUP // V7X_LANES):
        ls = slice(col * V7X_LANES, (col + 1) * V7X_LANES)
        w_small, w_big = POOL_WINDOWS[2 * col], POOL_WINDOWS[2 * col + 1]

        def window_sum(lo, hi, ls=ls):
            acc = p_s[pl.ds(HALO + lo, tile), ls]
            for off in range(lo + 1, hi):
                acc = acc + p_s[pl.ds(HALO + off, tile), ls]
            return acc

        inner = window_sum(-(w_small // 2), w_small // 2)
        outer = inner + window_sum(-(w_big // 2), -(w_small // 2)) + window_sum(w_small // 2, w_big // 2)
        half = jnp.where(first_half, w_small // 2, w_big // 2)
        cnt = (jnp.minimum(pos + half, seq) - jnp.maximum(pos - half, 0)).astype(F32)
        mean = jnp.where(first_half, inner, outer) / cnt
        d_cols.append(mean - p_s[pl.ds(HALO, tile), ls])
    d = jnp.concatenate(d_cols, axis=1).astype(BF16)
    y_pool = _dot(d, pool_w_ref[...]) * pool_scale_ref[...]

    y_four = _dot(spec_ref[...], four_w_ref[...])

    mu = _dot(v.astype(BF16), mavg_ref[...])
    vc = v - mu
    var = _dot((vc * vc).astype(BF16), mavg_ref[...])
    vh = vc * lax.rsqrt(var + EPS)
    lane_g = lax.broadcasted_iota(jnp.int32, (1, GROUP), 1)
    s_chunks = []
    for n in range(tile // CHUNK):
        vn = vh[n * CHUNK:(n + 1) * CHUNK]
        stack = jnp.concatenate(
            [jnp.where(lane_g // HEAD_DIM == hd, vn, 0.0).astype(BF16) for hd in range(N_HEADS)], axis=0)
        s_chunks.append(_dot(wcat_ref[...], stack) + sbias_ref[...])
    y_gmlp = u * jnp.concatenate(s_chunks, axis=0)

    y = jnp.concatenate(
        [_rms(yk, g_grp_ref[:, k * GROUP:(k + 1) * GROUP]).astype(BF16)
         for k, yk in enumerate((y_conv, y_pool, y_four, y_gmlp))], axis=1)
    m = _dot(y, w_out_ref[...])
    x1 = x + _rms(m, g_post_ref[...])

    h2 = _rms(x1, g_pre_ffn_ref[...]).astype(BF16)
    for j in range(D_FF // FF_CHUNK):
        ls = slice(j * FF_CHUNK, (j + 1) * FF_CHUNK)
        gate = _dot(h2, wg_ref[:, ls])
        up = _dot(h2, wu_ref[:, ls])
        act_s[:, ls] = (gate * jax.nn.sigmoid(gate) * up).astype(BF16)
    f = _dot(act_s[...], wd_ref[...])
    o_ref[...] = x1 + _rms(f, g_post_ffn_ref[...])


def _mixer_ffn(x2, spec, seq, p):
    n_tok = x2.shape[0]
    tile = min(TOKEN_TILE, seq)
    n_halo_blocks = n_tok // HALO
    per_tile = tile // HALO
    row = lambda shape: _const_spec(shape)
    return pl.pallas_call(
        functools.partial(_mixer_ffn_kernel, tile=tile, seq=seq),
        grid=(n_tok // tile,),
        in_specs=[
            pl.BlockSpec((tile, D_MODEL), lambda i: (i, 0)),
            pl.BlockSpec((HALO, D_MODEL), lambda i: (jnp.maximum(i * per_tile - 1, 0), 0)),
            pl.BlockSpec((HALO, D_MODEL), lambda i: (jnp.minimum((i + 1) * per_tile, n_halo_blocks - 1), 0)),
            pl.BlockSpec((tile, GROUP), lambda i: (i, 0)),
            row((1, D_MODEL)), row((1, D_MODEL)), row((1, D_MODEL)), row((1, D_MODEL)), row((1, D_MODEL)),
            row((D_MODEL, 6 * GROUP)),
            row((3, GROUP)),
            row((GROUP, GROUP)),
            row((1, GROUP)),
            row((GROUP, GROUP)),
            row((GROUP, GROUP)),
            row((CHUNK, N_HEADS * CHUNK)),
            row((CHUNK, GROUP)),
            row((D_MODEL, D_MODEL)),
            row((D_MODEL, D_FF)),
            row((D_MODEL, D_FF)),
            row((D_FF, D_MODEL)),
        ],
        out_specs=pl.BlockSpec((tile, D_MODEL), lambda i: (i, 0)),
        out_shape=jax.ShapeDtypeStruct(x2.shape, x2.dtype),
        scratch_shapes=[
            pltpu.VMEM((tile + 2 * HALO, GROUP), F32),
            pltpu.VMEM((tile + 2 * HALO, GROUP), F32),
            pltpu.VMEM((tile, D_FF), BF16),
        ],
        compiler_params=pltpu.CompilerParams(
            dimension_semantics=("arbitrary",),
            vmem_limit_bytes=V7X_VMEM_LIMIT_BYTES),
        name="mixer_ffn",
    )(x2, x2, x2, spec, p["g_pre"], p["g_post"], p["g_pre_ffn"], p["g_post_ffn"], p["g_grp"],
      p["w_in"], p["conv_w"], p["pool_w"], p["pool_scale"], p["four_w"], p["mavg"], p["wcat"],
      p["sbias"], p["w_out"], p["wg"], p["wu"], p["wd"])


def _block_diag(blocks):
    return jax.scipy.linalg.block_diag(*[blocks[i] for i in range(blocks.shape[0])])


def _layer_params(l, pre_mix_gain, post_mix_gain, pre_ffn_gain, post_ffn_gain, w_in, conv_w, pool_w,
                  pool_scale, fourier_w, spatial_w, spatial_b, group_norm_gain, w_out, w_gate, w_up,
                  w_down):
    wi = w_in[l]
    sl = lambda k: wi[:, k * GROUP:(k + 1) * GROUP]
    w_in6 = jnp.concatenate([sl(1), sl(2), sl(3), sl(0), sl(5), sl(6)], axis=1).astype(BF16)
    mavg = np.kron(np.eye(N_HEADS), np.full((HEAD_DIM, HEAD_DIM), 1.0 / HEAD_DIM))
    return dict(
        g_pre=pre_mix_gain[l][None], g_post=post_mix_gain[l][None],
        g_pre_ffn=pre_ffn_gain[l][None], g_post_ffn=post_ffn_gain[l][None],
        g_grp=group_norm_gain[l][None],
        w_in=w_in6, w_f=sl(4).astype(BF16),
        conv_w=conv_w[l],
        pool_w=_block_diag(pool_w[l]).astype(BF16),
        pool_scale=pool_scale[l][None],
        four_w=_block_diag(fourier_w[l]).astype(BF16),
        mavg=jnp.asarray(mavg, F32).astype(BF16),
        wcat=jnp.transpose(spatial_w[l], (1, 0, 2)).reshape(CHUNK, N_HEADS * CHUNK).astype(BF16),
        sbias=jnp.repeat(spatial_b[l].T, HEAD_DIM, axis=1),
        w_out=w_out[l].astype(BF16),
        wg=w_gate[l].astype(BF16), wu=w_up[l].astype(BF16), wd=w_down[l].astype(BF16),
    )


def kernel(x, pre_mix_gain, post_mix_gain, pre_ffn_gain, post_ffn_gain, w_in, conv_w, pool_w, pool_scale,
           fourier_w, spatial_w, spatial_b, group_norm_gain, w_out, w_gate, w_up, w_down):
    b, seq, d = x.shape
    assert d == D_MODEL and seq % SEQ_LO == 0 and seq % min(TOKEN_TILE, seq) == 0
    tables = _fourier_tables(seq)
    depth = w_in.shape[0]
    for l in range(depth):
        p = _layer_params(l, pre_mix_gain, post_mix_gain, pre_ffn_gain, post_ffn_gain, w_in, conv_w,
                          pool_w, pool_scale, fourier_w, spatial_w, spatial_b, group_norm_gain, w_out,
                          w_gate, w_up, w_down)
        spec = _fourier_spec(x, p["g_pre"], p["w_f"], tables)
        x = _mixer_ffn(x.reshape(b * seq, d), spec, seq, p).reshape(b, seq, d)
    return x
```

```python
import functools

import jax
import jax.numpy as jnp
import numpy as np
from jax import lax
from jax.experimental import pallas as pl
from jax.experimental.pallas import tpu as pltpu

D_MODEL = 1024
GROUP = 256
HEAD_DIM = 64
N_HEADS = GROUP // HEAD_DIM
CHUNK = 128
D_FF = 2816
POOL_WINDOWS = (2, 4, 8, 16)
EPS = 1e-6

V7X_LANES = 128
V7X_SUBLANES = 8
V7X_MXU_COLS = 256
V7X_VMEM_LIMIT_BYTES = 56 * 1024 * 1024

SEQ_LO = 128
HALO = 16
TOKEN_TILE = 512
FF_CHUNK = V7X_MXU_COLS
FOURIER_SUB = 2
FOURIER_OUT_LANES = 8 * GROUP

F32 = jnp.float32
BF16 = jnp.bfloat16


def _dot(a, b):
    return jnp.dot(a, b, preferred_element_type=F32)


def _rms(x, g):
    ms = jnp.mean(x * x, axis=-1, keepdims=True)
    return x * lax.rsqrt(ms + EPS) * g


def _fourier_tables(seq):
    seq_hi = seq // SEQ_LO
    cm = np.outer(np.arange(HEAD_DIM), np.arange(HEAD_DIM)) % HEAD_DIM
    ang = 2.0 * np.pi * cm / HEAD_DIM
    fc = np.zeros((GROUP, 2 * GROUP))
    for h in range(N_HEADS):
        sl = slice(h * HEAD_DIM, (h + 1) * HEAD_DIM)
        fc[sl, sl] = np.cos(ang) / np.sqrt(HEAD_DIM)
        fc[sl, GROUP + h * HEAD_DIM:GROUP + (h + 1) * HEAD_DIM] = -np.sin(ang) / np.sqrt(HEAD_DIM)
    kt = np.outer(np.arange(seq_hi), np.arange(seq_hi)) % seq_hi
    ang1 = 2.0 * np.pi * kt / seq_hi
    eye = np.eye(V7X_SUBLANES)
    c1 = np.kron(np.cos(ang1) / np.sqrt(seq_hi), eye)
    s1 = np.kron(np.sin(ang1) / np.sqrt(seq_hi), eye)
    kd = np.block([[c1, s1], [-s1, c1]])
    t_lo = np.arange(SEQ_LO).reshape(SEQ_LO // V7X_SUBLANES, 1, V7X_SUBLANES)
    k_lo = np.arange(seq_hi).reshape(1, seq_hi, 1)
    angt = 2.0 * np.pi * ((t_lo * k_lo) % seq) / seq
    angt = angt.reshape(SEQ_LO // V7X_SUBLANES, seq_hi * V7X_SUBLANES, 1)
    twc = np.broadcast_to(np.cos(angt), angt.shape[:2] + (V7X_LANES,))
    tws = np.broadcast_to(np.sin(angt), angt.shape[:2] + (V7X_LANES,))
    k2 = np.outer(np.arange(SEQ_LO), np.arange(SEQ_LO)) % SEQ_LO
    ang2 = 2.0 * np.pi * k2 / SEQ_LO
    e = np.concatenate([np.cos(ang2), np.sin(ang2)], axis=1) / np.sqrt(SEQ_LO)
    return (jnp.asarray(fc, F32).astype(BF16), jnp.asarray(kd, F32).astype(BF16), jnp.asarray(twc, F32),
            jnp.asarray(tws, F32), jnp.asarray(e, F32).astype(BF16))


def _fourier_kernel(x_ref, g_ref, wf_ref, fc_ref, kd_ref, twc_ref, tws_ref, e_ref, o_ref, y_s,
                    *, seq_hi, out_chunk):
    tb = pl.program_id(1)
    rows = seq_hi * V7X_SUBLANES

    def stage_norm(q):
        x = x_ref[0, :, q * V7X_SUBLANES:(q + 1) * V7X_SUBLANES, :].reshape(rows, D_MODEL)
        inv = lax.rsqrt(jnp.mean(x * x, axis=-1, keepdims=True) + EPS)
        return (x * g_ref[...]).astype(BF16), inv

    def stage_proj(q, h_inv):
        h, inv = h_inv
        return (_dot(h, wf_ref[...]) * inv).astype(BF16)

    def stage_channel_dft(q, f):
        g = _dot(f, fc_ref[...])
        return jnp.concatenate([g[:, :GROUP], g[:, GROUP:]], axis=0).astype(BF16)

    def stage_slow_dft(q, gs):
        return _dot(kd_ref[...], gs)

    def stage_twiddle_store(q, y):
        yr, yi = y[:rows], y[rows:]
        c = jnp.concatenate([twc_ref[q]] * (GROUP // V7X_LANES), axis=1)
        s = jnp.concatenate([tws_ref[q]] * (GROUP // V7X_LANES), axis=1)
        yr2 = yr * c + yi * s
        yi2 = yi * c - yr * s
        row = pl.multiple_of((tb * FOURIER_SUB + q) * V7X_SUBLANES, V7X_SUBLANES)
        for k in range(seq_hi):
            rs = slice(k * V7X_SUBLANES, (k + 1) * V7X_SUBLANES)
            ls = slice(k * GROUP, (k + 1) * GROUP)
            y_s[pl.ds(row, V7X_SUBLANES), ls] = yr2[rs]
            y_s[pl.ds(SEQ_LO + row, V7X_SUBLANES), ls] = yi2[rs]

    stages = (stage_norm, stage_proj, stage_channel_dft, stage_slow_dft, stage_twiddle_store)
    vals = [None] * FOURIER_SUB
    for t in range(len(stages) + FOURIER_SUB - 1):
        for q in range(FOURIER_SUB):
            k = t - q
            if 0 <= k < len(stages):
                vals[q] = stages[k](q) if k == 0 else stages[k](q, vals[q])

    @pl.when(tb == pl.num_programs(1) - 1)
    def _():
        for k in range(0, seq_hi * GROUP, out_chunk):
            ls = slice(k, k + out_chunk)
            o_ref[0, :, ls] = _dot(e_ref[...], y_s[:, ls].astype(BF16)).astype(o_ref.dtype)


def _const_spec(shape):
    n = len(shape)
    return pl.BlockSpec(shape, lambda *_: (0,) * n, pipeline_mode=pl.Buffered(1))


def _fourier_spec(x, gain, wf, tables):
    b, seq, _ = x.shape
    seq_hi = seq // SEQ_LO
    t_lo_per_step = V7X_SUBLANES * FOURIER_SUB
    n_tb = SEQ_LO // t_lo_per_step
    rows = seq_hi * V7X_SUBLANES
    fc, kd, twc, tws, e = tables
    out_chunk = min(seq_hi * GROUP, FOURIER_OUT_LANES)
    x4 = x.reshape(b, seq_hi, SEQ_LO, D_MODEL)
    out = pl.pallas_call(
        functools.partial(_fourier_kernel, seq_hi=seq_hi, out_chunk=out_chunk),
        grid=(b, n_tb),
        in_specs=[
            pl.BlockSpec((1, seq_hi, t_lo_per_step, D_MODEL), lambda i, t: (i, 0, t, 0)),
            _const_spec((1, D_MODEL)),
            _const_spec((D_MODEL, GROUP)),
            _const_spec((GROUP, 2 * GROUP)),
            _const_spec((2 * rows, 2 * rows)),
            pl.BlockSpec((FOURIER_SUB, rows, V7X_LANES), lambda i, t: (t, 0, 0)),
            pl.BlockSpec((FOURIER_SUB, rows, V7X_LANES), lambda i, t: (t, 0, 0)),
            _const_spec((SEQ_LO, 2 * SEQ_LO)),
        ],
        out_specs=pl.BlockSpec((1, SEQ_LO, seq_hi * GROUP), lambda i, t: (i, 0, 0)),
        out_shape=jax.ShapeDtypeStruct((b, SEQ_LO, seq_hi * GROUP), BF16),
        scratch_shapes=[pltpu.VMEM((2 * SEQ_LO, seq_hi * GROUP), F32)],
        compiler_params=pltpu.CompilerParams(
            dimension_semantics=("arbitrary", "arbitrary"),
            vmem_limit_bytes=V7X_VMEM_LIMIT_BYTES),
        name="fourier_spec",
    )(x4, gain, wf, fc, kd, twc, tws, e)
    return out.reshape(b * seq, GROUP)


def _mixer_ffn_kernel(x_ref, xp_ref, xn_ref, spec_ref, g_pre_ref, g_post_ref, g_pre_ffn_ref,
                      g_post_ffn_ref, g_grp_ref, w_in_ref, conv_w_ref, pool_w_ref, pool_scale_ref,
                      four_w_ref, mavg_ref, wcat_ref, sbias_ref, w_out_ref, wg_ref, wu_ref, wd_ref,
                      o_ref, cz_s, p_s, act_s, x1_s, h2_s, *, tile, seq, n_tiles):
    step = pl.program_id(0)

    @pl.when(step == 0)
    def _():
        x1_s[...] = jnp.zeros_like(x1_s)
        h2_s[...] = jnp.zeros_like(h2_s)

    def ffn_chunks(lo, hi):
        for j in range(lo, hi):
            ls = slice(j * FF_CHUNK, (j + 1) * FF_CHUNK)
            gate = _dot(h2_s[...], wg_ref[:, ls])
            up = _dot(h2_s[...], wu_ref[:, ls])
            act_s[:, ls] = (gate * jax.nn.sigmoid(gate) * up).astype(BF16)

    def ffn_down(rows):
        f = _dot(act_s[rows, :], wd_ref[...])
        o_ref[rows, :] = o_ref[rows, :] + _rms(f, g_post_ffn_ref[...])

    o_ref[...] = x1_s[...]
    ffn_chunks(0, 1)

    tiles_per_seq = seq // tile
    ti = jnp.minimum(step, n_tiles - 1) % tiles_per_seq
    x = x_ref[...]
    xh = jnp.concatenate([xp_ref[...], xn_ref[...]], axis=0)

    hm = _rms(x, g_pre_ref[...]).astype(BF16)
    hh = _rms(xh, g_pre_ref[...]).astype(BF16)
    ffn_chunks(1, 2)
    pm = _dot(hm, w_in_ref[...])
    ph = _dot(hh, w_in_ref[:, :3 * GROUP])
    ffn_chunks(2, 3)
    hrow = lax.broadcasted_iota(jnp.int32, (2 * HALO, 1), 0)
    has_prev = (ti != 0).astype(jnp.int32)
    has_next = (ti != tiles_per_seq - 1).astype(jnp.int32)
    ph = jnp.where(jnp.where(hrow < HALO, has_prev, has_next) != 0, ph, 0.0)

    cz_s[pl.ds(0, HALO), :] = ph[:HALO, :GROUP] * ph[:HALO, GROUP:2 * GROUP]
    cz_s[pl.ds(HALO, tile), :] = pm[:, :GROUP] * pm[:, GROUP:2 * GROUP]
    cz_s[pl.ds(HALO + tile, HALO), :] = ph[HALO:, :GROUP] * ph[HALO:, GROUP:2 * GROUP]
    p_s[pl.ds(0, HALO), :] = ph[:HALO, 2 * GROUP:]
    p_s[pl.ds(HALO, tile), :] = pm[:, 2 * GROUP:3 * GROUP]
    p_s[pl.ds(HALO + tile, HALO), :] = ph[HALO:, 2 * GROUP:]
    b_gate = pm[:, 3 * GROUP:4 * GROUP]
    u = pm[:, 4 * GROUP:5 * GROUP]
    v = pm[:, 5 * GROUP:]

    y_conv = b_gate * (conv_w_ref[0:1, :] * cz_s[pl.ds(HALO - 1, tile), :]
                       + conv_w_ref[1:2, :] * cz_s[pl.ds(HALO, tile), :]
                       + conv_w_ref[2:3, :] * cz_s[pl.ds(HALO + 1, tile), :])
    ffn_chunks(3, 4)

    pos = ti * tile + lax.broadcasted_iota(jnp.int32, (tile, 1), 0)
    lane = lax.broadcasted_iota(jnp.int32, (1, V7X_LANES), 1)
    first_half = lane < HEAD_DIM
    d_cols = []
    for col in range(GROUP // V7X_LANES):
        ls = slice(col * V7X_LANES, (col + 1) * V7X_LANES)
        w_small, w_big = POOL_WINDOWS[2 * col], POOL_WINDOWS[2 * col + 1]

        def window_sum(lo, hi, ls=ls):
            acc = p_s[pl.ds(HALO + lo, tile), ls]
            for off in range(lo + 1, hi):
                acc = acc + p_s[pl.ds(HALO + off, tile), ls]
            return acc

        inner = window_sum(-(w_small // 2), w_small // 2)
        outer = inner + window_sum(-(w_big // 2), -(w_small // 2)) + window_sum(w_small // 2, w_big // 2)
        half = jnp.where(first_half, w_small // 2, w_big // 2)
        cnt = (jnp.minimum(pos + half, seq) - jnp.maximum(pos - half, 0)).astype(F32)
        mean = jnp.where(first_half, inner, outer) / cnt
        d_cols.append(mean - p_s[pl.ds(HALO, tile), ls])
    d = jnp.concatenate(d_cols, axis=1).astype(BF16)
    y_pool = _dot(d, pool_w_ref[...]) * pool_scale_ref[...]

    y_four = _dot(spec_ref[...], four_w_ref[...])
    ffn_chunks(4, 5)

    mu = _dot(v.astype(BF16), mavg_ref[...])
    vc = v - mu
    var = _dot((vc * vc).astype(BF16), mavg_ref[...])
    vh = vc * lax.rsqrt(var + EPS)
    ffn_chunks(5, 6)
    lane_g = lax.broadcasted_iota(jnp.int32, (1, GROUP), 1)
    s_chunks = []
    for n in range(tile // CHUNK):
        vn = vh[n * CHUNK:(n + 1) * CHUNK]
        stack = jnp.concatenate(
            [jnp.where(lane_g // HEAD_DIM == hd, vn, 0.0).astype(BF16) for hd in range(N_HEADS)], axis=0)
        s_chunks.append(_dot(wcat_ref[...], stack) + sbias_ref[...])
    y_gmlp = u * jnp.concatenate(s_chunks, axis=0)
    ffn_chunks(6, 7)

    y = jnp.concatenate(
        [_rms(yk, g_grp_ref[:, k * GROUP:(k + 1) * GROUP]).astype(BF16)
         for k, yk in enumerate((y_conv, y_pool, y_four, y_gmlp))], axis=1)
    ffn_chunks(7, 8)
    m = _dot(y, w_out_ref[...])
    ffn_chunks(8, D_FF // FF_CHUNK)
    ffn_down(pl.ds(0, tile // 2))
    x1 = x + _rms(m, g_post_ref[...])
    h2 = _rms(x1, g_pre_ffn_ref[...]).astype(BF16)
    x1_s[...] = x1
    h2_s[...] = h2
    ffn_down(pl.ds(tile // 2, tile // 2))


def _mixer_ffn(x2, spec, seq, p):
    n_tok = x2.shape[0]
    tile = min(TOKEN_TILE, seq)
    n_tiles = n_tok // tile
    n_halo_blocks = n_tok // HALO
    per_tile = tile // HALO
    row = lambda shape: _const_spec(shape)
    cur = lambda i: jnp.minimum(i, n_tiles - 1)
    return pl.pallas_call(
        functools.partial(_mixer_ffn_kernel, tile=tile, seq=seq, n_tiles=n_tiles),
        grid=(n_tiles + 1,),
        in_specs=[
            pl.BlockSpec((tile, D_MODEL), lambda i: (cur(i), 0)),
            pl.BlockSpec((HALO, D_MODEL), lambda i: (jnp.maximum(cur(i) * per_tile - 1, 0), 0)),
            pl.BlockSpec((HALO, D_MODEL),
                         lambda i: (jnp.minimum((cur(i) + 1) * per_tile, n_halo_blocks - 1), 0)),
            pl.BlockSpec((tile, GROUP), lambda i: (cur(i), 0)),
            row((1, D_MODEL)), row((1, D_MODEL)), row((1, D_MODEL)), row((1, D_MODEL)), row((1, D_MODEL)),
            row((D_MODEL, 6 * GROUP)),
            row((3, GROUP)),
            row((GROUP, GROUP)),
            row((1, GROUP)),
            row((GROUP, GROUP)),
            row((GROUP, GROUP)),
            row((CHUNK, N_HEADS * CHUNK)),
            row((CHUNK, GROUP)),
            row((D_MODEL, D_MODEL)),
            row((D_MODEL, D_FF)),
            row((D_MODEL, D_FF)),
            row((D_FF, D_MODEL)),
        ],
        out_specs=pl.BlockSpec((tile, D_MODEL), lambda i: (jnp.maximum(i - 1, 0), 0)),
        out_shape=jax.ShapeDtypeStruct(x2.shape, x2.dtype),
        scratch_shapes=[
            pltpu.VMEM((tile + 2 * HALO, GROUP), F32),
            pltpu.VMEM((tile + 2 * HALO, GROUP), F32),
            pltpu.VMEM((tile, D_FF), BF16),
            pltpu.VMEM((tile, D_MODEL), F32),
            pltpu.VMEM((tile, D_MODEL), BF16),
        ],
        compiler_params=pltpu.CompilerParams(
            dimension_semantics=("arbitrary",),
            vmem_limit_bytes=V7X_VMEM_LIMIT_BYTES),
        name="mixer_ffn",
    )(x2, x2, x2, spec, p["g_pre"], p["g_post"], p["g_pre_ffn"], p["g_post_ffn"], p["g_grp"],
      p["w_in"], p["conv_w"], p["pool_w"], p["pool_scale"], p["four_w"], p["mavg"], p["wcat"],
      p["sbias"], p["w_out"], p["wg"], p["wu"], p["wd"])


def _block_diag(blocks):
    return jax.scipy.linalg.block_diag(*[blocks[i] for i in range(blocks.shape[0])])


def _layer_params(l, pre_mix_gain, post_mix_gain, pre_ffn_gain, post_ffn_gain, w_in, conv_w, pool_w,
                  pool_scale, fourier_w, spatial_w, spatial_b, group_norm_gain, w_out, w_gate, w_up,
                  w_down):
    wi = w_in[l]
    sl = lambda k: wi[:, k * GROUP:(k + 1) * GROUP]
    w_in6 = jnp.concatenate([sl(1), sl(2), sl(3), sl(0), sl(5), sl(6)], axis=1).astype(BF16)
    mavg = np.kron(np.eye(N_HEADS), np.full((HEAD_DIM, HEAD_DIM), 1.0 / HEAD_DIM))
    return dict(
        g_pre=pre_mix_gain[l][None], g_post=post_mix_gain[l][None],
        g_pre_ffn=pre_ffn_gain[l][None], g_post_ffn=post_ffn_gain[l][None],
        g_grp=group_norm_gain[l][None],
        w_in=w_in6, w_f=sl(4).astype(BF16),
        conv_w=conv_w[l],
        pool_w=_block_diag(pool_w[l]).astype(BF16),
        pool_scale=pool_scale[l][None],
        four_w=_block_diag(fourier_w[l]).astype(BF16),
        mavg=jnp.asarray(mavg, F32).astype(BF16),
        wcat=jnp.transpose(spatial_w[l], (1, 0, 2)).reshape(CHUNK, N_HEADS * CHUNK).astype(BF16),
        sbias=jnp.repeat(spatial_b[l].T, HEAD_DIM, axis=1),
        w_out=w_out[l].astype(BF16),
        wg=w_gate[l].astype(BF16), wu=w_up[l].astype(BF16), wd=w_down[l].astype(BF16),
    )


def kernel(x, pre_mix_gain, post_mix_gain, pre_ffn_gain, post_ffn_gain, w_in, conv_w, pool_w, pool_scale,
           fourier_w, spatial_w, spatial_b, group_norm_gain, w_out, w_gate, w_up, w_down):
    b, seq, d = x.shape
    assert d == D_MODEL and seq % SEQ_LO == 0 and seq % min(TOKEN_TILE, seq) == 0
    tables = _fourier_tables(seq)
    depth = w_in.shape[0]
    for l in range(depth):
        p = _layer_params(l, pre_mix_gain, post_mix_gain, pre_ffn_gain, post_ffn_gain, w_in, conv_w,
                          pool_w, pool_scale, fourier_w, spatial_w, spatial_b, group_norm_gain, w_out,
                          w_gate, w_up, w_down)
        spec = _fourier_spec(x, p["g_pre"], p["w_f"], tables)
        x = _mixer_ffn(x.reshape(b * seq, d), spec, seq, p).reshape(b, seq, d)
    return x
```

```python
import functools

import jax
import jax.numpy as jnp
import numpy as np
from jax import lax
from jax.experimental import pallas as pl
from jax.experimental.pallas import tpu as pltpu

D_MODEL = 1024
GROUP = 256
HEAD_DIM = 64
N_HEADS = GROUP // HEAD_DIM
CHUNK = 128
D_FF = 2816
POOL_WINDOWS = (2, 4, 8, 16)
EPS = 1e-6

V7X_LANES = 128
V7X_SUBLANES = 8
V7X_MXU_COLS = 256
V7X_VMEM_LIMIT_BYTES = 56 * 1024 * 1024

SEQ_LO = 128
HALO = 16
TOKEN_TILE = 512
FF_CHUNK = V7X_MXU_COLS
FOURIER_SUB = 2

F32 = jnp.float32
BF16 = jnp.bfloat16


def _dot(a, b):
    return jnp.dot(a, b, preferred_element_type=F32)


def _rms(x, g):
    ms = jnp.mean(x * x, axis=-1, keepdims=True)
    return x * lax.rsqrt(ms + EPS) * g


def _fourier_tables(seq):
    seq_hi = seq // SEQ_LO
    cm = np.outer(np.arange(HEAD_DIM), np.arange(HEAD_DIM)) % HEAD_DIM
    ang = 2.0 * np.pi * cm / HEAD_DIM
    cos_bd = np.kron(np.eye(N_HEADS), np.cos(ang)) / np.sqrt(HEAD_DIM)
    sin_bd = np.kron(np.eye(N_HEADS), np.sin(ang)) / np.sqrt(HEAD_DIM)
    fc = np.block([[cos_bd, -sin_bd], [sin_bd, cos_bd]])
    kt = np.outer(np.arange(seq_hi), np.arange(seq_hi)) % seq_hi
    ang1 = 2.0 * np.pi * kt / seq_hi
    eye = np.eye(V7X_SUBLANES)
    c1 = np.kron(np.cos(ang1) / np.sqrt(seq_hi), eye)
    s1 = np.kron(np.sin(ang1) / np.sqrt(seq_hi), eye)
    kd = np.concatenate([c1, -s1], axis=0)
    t_lo = np.arange(SEQ_LO).reshape(SEQ_LO // V7X_SUBLANES, 1, V7X_SUBLANES)
    k_lo = np.arange(seq_hi).reshape(1, seq_hi, 1)
    angt = 2.0 * np.pi * ((t_lo * k_lo) % seq) / seq
    angt = angt.reshape(SEQ_LO // V7X_SUBLANES, seq_hi * V7X_SUBLANES, 1)
    twc = np.broadcast_to(np.cos(angt), angt.shape[:2] + (V7X_LANES,))
    tws = np.broadcast_to(np.sin(angt), angt.shape[:2] + (V7X_LANES,))
    k2 = np.outer(np.arange(SEQ_LO), np.arange(SEQ_LO)) % SEQ_LO
    ang2 = 2.0 * np.pi * k2 / SEQ_LO
    e = np.concatenate([np.cos(ang2), np.sin(ang2)], axis=1) / np.sqrt(SEQ_LO)
    return (jnp.asarray(fc, F32).astype(BF16), jnp.asarray(kd, F32).astype(BF16), jnp.asarray(twc, F32),
            jnp.asarray(tws, F32), jnp.asarray(e, F32).astype(BF16))


def _fourier_kernel(x_ref, g_ref, wf_ref, fc_ref, kd_ref, twc_ref, tws_ref, e_ref, o_ref, y_s, il_s,
                    *, seq_hi):
    tb = pl.program_id(1)
    rows = seq_hi * V7X_SUBLANES
    t_lo_per_step = V7X_SUBLANES * FOURIER_SUB

    def stage_norm(q):
        x = x_ref[0, :, q * V7X_SUBLANES:(q + 1) * V7X_SUBLANES, :].reshape(rows, D_MODEL)
        inv = lax.rsqrt(jnp.mean(x * x, axis=-1, keepdims=True) + EPS)
        return (x * g_ref[...]).astype(BF16), inv

    def stage_proj(q, h_inv):
        h, inv = h_inv
        return (_dot(h, wf_ref[...]) * inv).astype(BF16)

    def stage_slow_dft(q, f):
        return _dot(kd_ref[...], f)

    def stage_twiddle(q, u):
        ur, ui = u[:rows], u[rows:]
        c = jnp.concatenate([twc_ref[q]] * (GROUP // V7X_LANES), axis=1)
        s = jnp.concatenate([tws_ref[q]] * (GROUP // V7X_LANES), axis=1)
        return jnp.concatenate([ur * c + ui * s, ui * c - ur * s], axis=1).astype(BF16)

    def stage_channel_dft(q, uc):
        return _dot(uc, fc_ref[...])

    stages = (stage_norm, stage_proj, stage_slow_dft, stage_twiddle, stage_channel_dft)
    vals = [None] * FOURIER_SUB
    for t in range(len(stages) + FOURIER_SUB - 1):
        for q in range(FOURIER_SUB):
            k = t - q
            if 0 <= k < len(stages):
                vals[q] = stages[k](q) if k == 0 else stages[k](q, vals[q])

    row = pl.multiple_of(tb * t_lo_per_step, t_lo_per_step)
    for k in range(seq_hi):
        rs = slice(k * V7X_SUBLANES, (k + 1) * V7X_SUBLANES)
        ls = slice(k * GROUP, (k + 1) * GROUP)
        piece = jnp.concatenate([v[rs] for v in vals], axis=0).astype(BF16)
        y_s[pl.ds(row, t_lo_per_step), ls] = piece[:, :GROUP]
        y_s[pl.ds(SEQ_LO + row, t_lo_per_step), ls] = piece[:, GROUP:]

    @pl.when(tb == pl.num_programs(1) - 1)
    def _():
        def out_block(kb, carry):
            lanes = V7X_SUBLANES * GROUP
            res = _dot(e_ref[...], y_s[:, pl.ds(pl.multiple_of(kb * lanes, lanes), lanes)])
            for j in range(V7X_SUBLANES):
                for half in range(GROUP // V7X_LANES):
                    ls = slice(j * GROUP + half * V7X_LANES, j * GROUP + (half + 1) * V7X_LANES)
                    il_s[half, pl.ds(j, SEQ_LO, stride=V7X_SUBLANES), :] = res[:, ls]
            for half in range(GROUP // V7X_LANES):
                o_ref[half, 0, :, kb, :, :] = il_s[half].reshape(SEQ_LO, V7X_SUBLANES, V7X_LANES)
            return carry

        lax.fori_loop(0, seq_hi // V7X_SUBLANES, out_block, 0)


def _const_spec(shape):
    n = len(shape)
    return pl.BlockSpec(shape, lambda *_: (0,) * n, pipeline_mode=pl.Buffered(1))


def _fourier_spec(x, gain, wf, tables):
    b, seq, _ = x.shape
    seq_hi = seq // SEQ_LO
    t_lo_per_step = V7X_SUBLANES * FOURIER_SUB
    n_tb = SEQ_LO // t_lo_per_step
    rows = seq_hi * V7X_SUBLANES
    fc, kd, twc, tws, e = tables
    halves = GROUP // V7X_LANES
    k_lo_blocks = seq_hi // V7X_SUBLANES
    out_block = (halves, 1, SEQ_LO, k_lo_blocks, V7X_SUBLANES, V7X_LANES)
    x4 = x.reshape(b, seq_hi, SEQ_LO, D_MODEL)
    out = pl.pallas_call(
        functools.partial(_fourier_kernel, seq_hi=seq_hi),
        grid=(b, n_tb),
        in_specs=[
            pl.BlockSpec((1, seq_hi, t_lo_per_step, D_MODEL), lambda i, t: (i, 0, t, 0)),
            _const_spec((1, D_MODEL)),
            _const_spec((D_MODEL, GROUP)),
            _const_spec((2 * GROUP, 2 * GROUP)),
            _const_spec((2 * rows, rows)),
            pl.BlockSpec((FOURIER_SUB, rows, V7X_LANES), lambda i, t: (t, 0, 0)),
            pl.BlockSpec((FOURIER_SUB, rows, V7X_LANES), lambda i, t: (t, 0, 0)),
            _const_spec((SEQ_LO, 2 * SEQ_LO)),
        ],
        out_specs=pl.BlockSpec(out_block, lambda i, t: (0, i, 0, 0, 0, 0)),
        out_shape=jax.ShapeDtypeStruct((halves, b) + out_block[2:], F32),
        scratch_shapes=[
            pltpu.VMEM((2 * SEQ_LO, seq_hi * GROUP), BF16),
            pltpu.VMEM((halves, SEQ_LO * V7X_SUBLANES, V7X_LANES), F32),
        ],
        compiler_params=pltpu.CompilerParams(
            dimension_semantics=("arbitrary", "arbitrary"),
            vmem_limit_bytes=V7X_VMEM_LIMIT_BYTES),
        name="fourier_spec",
    )(x4, gain, wf, fc, kd, twc, tws, e)
    return out.reshape(halves, b * seq, V7X_LANES)


def _mixer_ffn_kernel(x_ref, xp_ref, xn_ref, spec_ref, gains_ref, vec_ref, mats_ref, wcat_ref, w_in_ref,
                      w_out_ref, wgu_ref, wd_ref,
                      o_ref, cz_s, p_s, ps_s, act_s, x1_s, h2_s, *, tile, seq, n_tiles):
    g_pre_ref, g_post_ref, g_pre_ffn_ref, g_post_ffn_ref, g_grp_ref = (
        gains_ref.at[k:k + 1] for k in range(5))
    sbias_ref = vec_ref.at[0:CHUNK]
    conv_w_ref = vec_ref.at[CHUNK:CHUNK + 3]
    pool_scale_ref = vec_ref.at[CHUNK + 3:CHUNK + 4]
    pool_w_ref, four_w_ref, mavg_ref = (mats_ref.at[k] for k in range(3))
    wg_ref = wgu_ref.at[:, 0:D_FF]
    wu_ref = wgu_ref.at[:, D_FF:2 * D_FF]
    step = pl.program_id(0)

    @pl.when(step == 0)
    def _():
        x1_s[...] = jnp.zeros_like(x1_s)
        h2_s[...] = jnp.zeros_like(h2_s)

    def ffn_chunks(lo, hi):
        for j in range(lo, hi):
            ls = slice(j * FF_CHUNK, (j + 1) * FF_CHUNK)
            gate = _dot(h2_s[...], wg_ref[:, ls])
            up = _dot(h2_s[...], wu_ref[:, ls])
            act_s[:, ls] = (gate * jax.nn.sigmoid(gate) * up).astype(BF16)

    def ffn_down(rows):
        f = _dot(act_s[rows, :], wd_ref[...])
        o_ref[rows, :] = o_ref[rows, :] + _rms(f, g_post_ffn_ref[...])

    o_ref[...] = x1_s[...]
    ffn_chunks(0, 1)

    tiles_per_seq = seq // tile
    ti = jnp.minimum(step, n_tiles - 1) % tiles_per_seq
    x = x_ref[...]
    xh = jnp.concatenate([xp_ref[...], xn_ref[...]], axis=0)

    hm = _rms(x, g_pre_ref[...]).astype(BF16)
    hh = _rms(xh, g_pre_ref[...]).astype(BF16)
    ffn_chunks(1, 2)
    pe = _dot(jnp.concatenate([hm, hh], axis=0), w_in_ref[:, :3 * GROUP])
    pm = jnp.concatenate([pe[:tile], _dot(hm, w_in_ref[:, 3 * GROUP:])], axis=1)
    ph = pe[tile:]
    ffn_chunks(2, 3)
    hrow = lax.broadcasted_iota(jnp.int32, (2 * HALO, 1), 0)
    has_prev = (ti != 0).astype(jnp.int32)
    has_next = (ti != tiles_per_seq - 1).astype(jnp.int32)
    ph = jnp.where(jnp.where(hrow < HALO, has_prev, has_next) != 0, ph, 0.0)

    cz_s[pl.ds(0, HALO), :] = ph[:HALO, :GROUP] * ph[:HALO, GROUP:2 * GROUP]
    cz_s[pl.ds(HALO, tile), :] = pm[:, :GROUP] * pm[:, GROUP:2 * GROUP]
    cz_s[pl.ds(HALO + tile, HALO), :] = ph[HALO:, :GROUP] * ph[HALO:, GROUP:2 * GROUP]
    p_s[pl.ds(0, HALO), :] = ph[:HALO, 2 * GROUP:]
    p_s[pl.ds(HALO, tile), :] = pm[:, 2 * GROUP:3 * GROUP]
    p_s[pl.ds(HALO + tile, HALO), :] = ph[HALO:, 2 * GROUP:]
    p_s[pl.ds(2 * HALO + tile, V7X_SUBLANES), :] = jnp.zeros((V7X_SUBLANES, GROUP), F32)
    b_gate = pm[:, 3 * GROUP:4 * GROUP]
    u = pm[:, 4 * GROUP:5 * GROUP]
    v = pm[:, 5 * GROUP:]

    y_conv = b_gate * (conv_w_ref[0:1, :] * cz_s[pl.ds(HALO - 1, tile), :]
                       + conv_w_ref[1:2, :] * cz_s[pl.ds(HALO, tile), :]
                       + conv_w_ref[2:3, :] * cz_s[pl.ds(HALO + 1, tile), :])
    ffn_chunks(3, 4)

    pos = ti * tile + lax.broadcasted_iota(jnp.int32, (tile, 1), 0)
    lane = lax.broadcasted_iota(jnp.int32, (1, V7X_LANES), 1)
    first_half = lane < HEAD_DIM
    def pool_col(col, inner, outer):
        w_small, w_big = POOL_WINDOWS[2 * col], POOL_WINDOWS[2 * col + 1]
        half = jnp.where(first_half, w_small // 2, w_big // 2)
        cnt = (jnp.minimum(pos + half, seq) - jnp.maximum(pos - half, 0)).astype(F32)
        mean = jnp.where(first_half, inner, outer) / cnt
        return mean - p_s[pl.ds(HALO, tile), col * V7X_LANES:(col + 1) * V7X_LANES]

    ls0 = slice(0, V7X_LANES)
    inner0 = p_s[pl.ds(HALO - 1, tile), ls0] + p_s[pl.ds(HALO, tile), ls0]
    outer0 = inner0 + p_s[pl.ds(HALO - 2, tile), ls0] + p_s[pl.ds(HALO + 1, tile), ls0]
    d_col0 = pool_col(0, inner0, outer0)
    ffn_chunks(4, 5)
    ls1 = slice(V7X_LANES, 2 * V7X_LANES)
    base = HALO - POOL_WINDOWS[3] // 2
    n2, n4, n8 = tile + 24, tile + 16, tile + 8
    ps_s[0, pl.ds(base, n2), :] = p_s[pl.ds(base, n2), ls1] + p_s[pl.ds(base + 1, n2), ls1]
    ps_s[1, pl.ds(base, n4), :] = ps_s[0, pl.ds(base, n4), :] + ps_s[0, pl.ds(base + 2, n4), :]
    ps_s[0, pl.ds(base, n8), :] = ps_s[1, pl.ds(base, n8), :] + ps_s[1, pl.ds(base + 4, n8), :]
    inner1 = ps_s[0, pl.ds(HALO - 4, tile), :]
    outer1 = ps_s[0, pl.ds(HALO - 8, tile), :] + ps_s[0, pl.ds(HALO, tile), :]
    d = jnp.concatenate([d_col0, pool_col(1, inner1, outer1)], axis=1).astype(BF16)
    y_pool = _dot(d, pool_w_ref[...]) * pool_scale_ref[...]

    spec = jnp.concatenate([spec_ref[k] for k in range(GROUP // V7X_LANES)], axis=1).astype(BF16)
    y_four = _dot(spec, four_w_ref[...])
    ffn_chunks(5, 6)

    mu = _dot(v.astype(BF16), mavg_ref[...])
    vc = v - mu
    var = _dot((vc * vc).astype(BF16), mavg_ref[...])
    vh = vc * lax.rsqrt(var + EPS)
    ffn_chunks(6, 7)
    lane_g = lax.broadcasted_iota(jnp.int32, (1, GROUP), 1)
    s_chunks = []
    for n in range(tile // CHUNK):
        vn = vh[n * CHUNK:(n + 1) * CHUNK]
        stack = jnp.concatenate(
            [jnp.where(lane_g // HEAD_DIM == hd, vn, 0.0).astype(BF16) for hd in range(N_HEADS)], axis=0)
        s_chunks.append(_dot(wcat_ref[...], stack) + sbias_ref[...])
    y_gmlp = u * jnp.concatenate(s_chunks, axis=0)
    ffn_chunks(7, 8)

    y = jnp.concatenate(
        [_rms(yk, g_grp_ref[:, k * GROUP:(k + 1) * GROUP]).astype(BF16)
         for k, yk in enumerate((y_conv, y_pool, y_four, y_gmlp))], axis=1)
    ffn_chunks(8, 9)

    def out_half(rows):
        m = _dot(y[rows, :], w_out_ref[...])
        x1 = x_ref[rows, :] + _rms(m, g_post_ref[...])
        return x1, _rms(x1, g_pre_ffn_ref[...]).astype(BF16)

    top, bot = slice(0, tile // 2), slice(tile // 2, tile)
    x1_top, h2_top = out_half(top)
    ffn_chunks(9, 10)
    x1_bot, h2_bot = out_half(bot)
    ffn_chunks(10, D_FF // FF_CHUNK)
    ffn_down(top)
    x1_s[top, :] = x1_top
    h2_s[top, :] = h2_top
    x1_s[bot, :] = x1_bot
    h2_s[bot, :] = h2_bot
    ffn_down(bot)


def _mixer_ffn(x2, spec, seq, p):
    n_tok = x2.shape[0]
    tile = min(TOKEN_TILE, seq)
    n_tiles = n_tok // tile
    n_halo_blocks = n_tok // HALO
    per_tile = tile // HALO
    row = lambda shape: _const_spec(shape)
    cur = lambda i: jnp.minimum(i, n_tiles - 1)
    return pl.pallas_call(
        functools.partial(_mixer_ffn_kernel, tile=tile, seq=seq, n_tiles=n_tiles),
        grid=(n_tiles + 1,),
        in_specs=[
            pl.BlockSpec((tile, D_MODEL), lambda i: (cur(i), 0)),
            pl.BlockSpec((HALO, D_MODEL), lambda i: (jnp.maximum(cur(i) * per_tile - 1, 0), 0)),
            pl.BlockSpec((HALO, D_MODEL),
                         lambda i: (jnp.minimum((cur(i) + 1) * per_tile, n_halo_blocks - 1), 0)),
            pl.BlockSpec((GROUP // V7X_LANES, tile, V7X_LANES), lambda i: (0, cur(i), 0)),
            row(p["gains"].shape),
            row(p["vec"].shape),
            row(p["mats"].shape),
            row((CHUNK, N_HEADS * CHUNK)),
            row((D_MODEL, 6 * GROUP)),
            row((D_MODEL, D_MODEL)),
            row((D_MODEL, 2 * D_FF)),
            row((D_FF, D_MODEL)),
        ],
        out_specs=pl.BlockSpec((tile, D_MODEL), lambda i: (jnp.maximum(i - 1, 0), 0)),
        out_shape=jax.ShapeDtypeStruct(x2.shape, x2.dtype),
        scratch_shapes=[
            pltpu.VMEM((tile + 2 * HALO, GROUP), F32),
            pltpu.VMEM((tile + 2 * HALO + V7X_SUBLANES, GROUP), F32),
            pltpu.VMEM((2, tile + 2 * HALO, V7X_LANES), F32),
            pltpu.VMEM((tile, D_FF), BF16),
            pltpu.VMEM((tile, D_MODEL), F32),
            pltpu.VMEM((tile, D_MODEL), BF16),
        ],
        compiler_params=pltpu.CompilerParams(
            dimension_semantics=("arbitrary",),
            vmem_limit_bytes=V7X_VMEM_LIMIT_BYTES),
        name="mixer_ffn",
    )(x2, x2, x2, spec, p["gains"], p["vec"], p["mats"], p["wcat"], p["w_in"], p["w_out"], p["wgu"], p["wd"])


def _block_diag(blocks):
    return jax.scipy.linalg.block_diag(*[blocks[i] for i in range(blocks.shape[0])])


def _layer_params(l, pre_mix_gain, post_mix_gain, pre_ffn_gain, post_ffn_gain, w_in, conv_w, pool_w,
                  pool_scale, fourier_w, spatial_w, spatial_b, group_norm_gain, w_out, w_gate, w_up,
                  w_down):
    wi = w_in[l]
    sl = lambda k: wi[:, k * GROUP:(k + 1) * GROUP]
    w_in6 = jnp.concatenate([sl(1), sl(2), sl(3), sl(0), sl(5), sl(6)], axis=1).astype(BF16)
    mavg = np.kron(np.eye(N_HEADS), np.full((HEAD_DIM, HEAD_DIM), 1.0 / HEAD_DIM))
    sbias = jnp.repeat(spatial_b[l].T, HEAD_DIM, axis=1)
    pad = jnp.zeros((V7X_SUBLANES - 5, D_MODEL), F32)
    return dict(
        g_pre=pre_mix_gain[l][None],
        gains=jnp.concatenate([pre_mix_gain[l][None], post_mix_gain[l][None], pre_ffn_gain[l][None],
                               post_ffn_gain[l][None], group_norm_gain[l][None], pad], axis=0),
        vec=jnp.concatenate([sbias, conv_w[l], pool_scale[l][None],
                             jnp.zeros((V7X_SUBLANES - 4, GROUP), F32)], axis=0),
        mats=jnp.stack([_block_diag(pool_w[l]), _block_diag(fourier_w[l]), jnp.asarray(mavg, F32)]).astype(BF16),
        wcat=jnp.transpose(spatial_w[l], (1, 0, 2)).reshape(CHUNK, N_HEADS * CHUNK).astype(BF16),
        w_in=w_in6, w_f=sl(4).astype(BF16),
        w_out=w_out[l].astype(BF16),
        wgu=jnp.concatenate([w_gate[l], w_up[l]], axis=1).astype(BF16),
        wd=w_down[l].astype(BF16),
    )


def kernel(x, pre_mix_gain, post_mix_gain, pre_ffn_gain, post_ffn_gain, w_in, conv_w, pool_w, pool_scale,
           fourier_w, spatial_w, spatial_b, group_norm_gain, w_out, w_gate, w_up, w_down):
    b, seq, d = x.shape
    assert d == D_MODEL and seq % SEQ_LO == 0 and seq % min(TOKEN_TILE, seq) == 0
    tables = _fourier_tables(seq)
    depth = w_in.shape[0]
    for l in range(depth):
        p = _layer_params(l, pre_mix_gain, post_mix_gain, pre_ffn_gain, post_ffn_gain, w_in, conv_w,
                          pool_w, pool_scale, fourier_w, spatial_w, spatial_b, group_norm_gain, w_out,
                          w_gate, w_up, w_down)
        spec = _fourier_spec(x, p["g_pre"], p["w_f"], tables)
        x = _mixer_ffn(x.reshape(b * seq, d), spec, seq, p).reshape(b, seq, d)
    return x
```

```python
import functools

import jax
import jax.numpy as jnp
import numpy as np
from jax import lax
from jax.experimental import pallas as pl
from jax.experimental.pallas import tpu as pltpu

D_MODEL = 1024
GROUP = 256
HEAD_DIM = 64
N_HEADS = GROUP // HEAD_DIM
CHUNK = 128
D_FF = 2816
POOL_WINDOWS = (2, 4, 8, 16)
EPS = 1e-6

V7X_LANES = 128
V7X_SUBLANES = 8
V7X_MXU_COLS = 256
V7X_VMEM_LIMIT_BYTES = 56 * 1024 * 1024

SEQ_LO = 128
HALO = 16
TOKEN_TILE = 512
FF_CHUNK = V7X_MXU_COLS
W_IN_COL = 0
W_F_COL = W_IN_COL + 6 * GROUP
W_OUT_COL = W_F_COL + GROUP
W_GATE_COL = W_OUT_COL + D_MODEL
W_UP_COL = W_GATE_COL + D_FF
W_ROWS_COLS = W_UP_COL + D_FF
FOURIER_SUB = 4

F32 = jnp.float32
BF16 = jnp.bfloat16


def _dot(a, b):
    return jnp.dot(a, b, preferred_element_type=F32)


def _rms(x, g):
    ms = jnp.mean(x * x, axis=-1, keepdims=True)
    return x * lax.rsqrt(ms + EPS) * g


def _fourier_tables(seq):
    seq_hi = seq // SEQ_LO
    cm = np.outer(np.arange(HEAD_DIM), np.arange(HEAD_DIM)) % HEAD_DIM
    ang = 2.0 * np.pi * cm / HEAD_DIM
    cos_bd = np.kron(np.eye(N_HEADS), np.cos(ang)) / np.sqrt(HEAD_DIM)
    sin_bd = np.kron(np.eye(N_HEADS), np.sin(ang)) / np.sqrt(HEAD_DIM)
    fc = np.block([[cos_bd, -sin_bd], [sin_bd, cos_bd]])
    kt = np.outer(np.arange(seq_hi), np.arange(seq_hi)) % seq_hi
    ang1 = 2.0 * np.pi * kt / seq_hi
    eye = np.eye(V7X_SUBLANES)
    c1 = np.kron(np.cos(ang1) / np.sqrt(seq_hi), eye)
    s1 = np.kron(np.sin(ang1) / np.sqrt(seq_hi), eye)
    kd = np.concatenate([c1, -s1], axis=0)
    t_lo = np.arange(SEQ_LO).reshape(SEQ_LO // V7X_SUBLANES, 1, V7X_SUBLANES)
    k_lo = np.arange(seq_hi).reshape(1, seq_hi, 1)
    angt = 2.0 * np.pi * ((t_lo * k_lo) % seq) / seq
    angt = angt.reshape(SEQ_LO // V7X_SUBLANES, seq_hi * V7X_SUBLANES, 1)
    twc = np.broadcast_to(np.cos(angt), angt.shape[:2] + (V7X_LANES,))
    tws = np.broadcast_to(np.sin(angt), angt.shape[:2] + (V7X_LANES,))
    k2 = np.outer(np.arange(SEQ_LO), np.arange(SEQ_LO)) % SEQ_LO
    ang2 = 2.0 * np.pi * k2 / SEQ_LO
    e = np.concatenate([np.cos(ang2), np.sin(ang2)], axis=1) / np.sqrt(SEQ_LO)
    return (jnp.asarray(fc, F32).astype(BF16), jnp.asarray(kd, F32).astype(BF16), jnp.asarray(twc, F32),
            jnp.asarray(tws, F32), jnp.asarray(e, F32).astype(BF16))


def _fourier_kernel(x_ref, g_ref, wf_ref, fc_ref, kd_ref, twc_ref, tws_ref, e_ref, o_ref, y_s, il_s,
                    *, seq_hi):
    tb = pl.program_id(1)
    rows = seq_hi * V7X_SUBLANES
    t_lo_per_step = V7X_SUBLANES * FOURIER_SUB

    def stage_norm(q):
        x = x_ref[0, :, q * V7X_SUBLANES:(q + 1) * V7X_SUBLANES, :].reshape(rows, D_MODEL)
        inv = lax.rsqrt(jnp.mean(x * x, axis=-1, keepdims=True) + EPS)
        return (x * g_ref[0:1, :]).astype(BF16), inv

    def stage_proj(q, h_inv):
        h, inv = h_inv
        return (_dot(h, wf_ref[...]) * inv).astype(BF16)

    def stage_slow_dft(q, f):
        return _dot(kd_ref[...], f)

    def stage_twiddle(q, u):
        ur, ui = u[:rows], u[rows:]
        c = jnp.concatenate([twc_ref[q]] * (GROUP // V7X_LANES), axis=1)
        s = jnp.concatenate([tws_ref[q]] * (GROUP // V7X_LANES), axis=1)
        return jnp.concatenate([ur * c + ui * s, ui * c - ur * s], axis=1).astype(BF16)

    def stage_channel_dft(q, uc):
        return _dot(uc, fc_ref[...])

    stages = (stage_norm, stage_proj, stage_slow_dft, stage_twiddle, stage_channel_dft)
    vals = [None] * FOURIER_SUB
    for t in range(len(stages) + FOURIER_SUB - 1):
        for q in range(FOURIER_SUB):
            k = t - q
            if 0 <= k < len(stages):
                vals[q] = stages[k](q) if k == 0 else stages[k](q, vals[q])

    row = pl.multiple_of(tb * t_lo_per_step, t_lo_per_step)
    for k in range(seq_hi):
        rs = slice(k * V7X_SUBLANES, (k + 1) * V7X_SUBLANES)
        ls = slice(k * GROUP, (k + 1) * GROUP)
        piece = jnp.concatenate([v[rs] for v in vals], axis=0).astype(BF16)
        y_s[pl.ds(row, t_lo_per_step), ls] = piece[:, :GROUP]
        y_s[pl.ds(SEQ_LO + row, t_lo_per_step), ls] = piece[:, GROUP:]

    @pl.when(tb == pl.num_programs(1) - 1)
    def _():
        def out_block(kb, carry):
            lanes = V7X_SUBLANES * GROUP
            res = _dot(e_ref[...], y_s[:, pl.ds(pl.multiple_of(kb * lanes, lanes), lanes)])
            for j in range(V7X_SUBLANES):
                for half in range(GROUP // V7X_LANES):
                    ls = slice(j * GROUP + half * V7X_LANES, j * GROUP + (half + 1) * V7X_LANES)
                    il_s[half, pl.ds(j, SEQ_LO, stride=V7X_SUBLANES), :] = res[:, ls]
            for half in range(GROUP // V7X_LANES):
                o_ref[half, 0, :, kb, :, :] = il_s[half].reshape(SEQ_LO, V7X_SUBLANES, V7X_LANES)
            return carry

        lax.fori_loop(0, seq_hi // V7X_SUBLANES, out_block, 0)


def _const_spec(shape):
    n = len(shape)
    return pl.BlockSpec(shape, lambda *_: (0,) * n, pipeline_mode=pl.Buffered(1))


def _layer_spec(layer, shape, block_index=None):
    index = (0,) * len(shape) if block_index is None else block_index
    return pl.BlockSpec((None,) + shape, lambda *_: (layer,) + index, pipeline_mode=pl.Buffered(1))


def _fourier_spec(x, layer, params, tables):
    b, seq, _ = x.shape
    seq_hi = seq // SEQ_LO
    t_lo_per_step = V7X_SUBLANES * FOURIER_SUB
    n_tb = SEQ_LO // t_lo_per_step
    rows = seq_hi * V7X_SUBLANES
    fc, kd, twc, tws, e = tables
    halves = GROUP // V7X_LANES
    k_lo_blocks = seq_hi // V7X_SUBLANES
    out_block = (halves, 1, SEQ_LO, k_lo_blocks, V7X_SUBLANES, V7X_LANES)
    x4 = x.reshape(b, seq_hi, SEQ_LO, D_MODEL)
    out = pl.pallas_call(
        functools.partial(_fourier_kernel, seq_hi=seq_hi),
        grid=(b, n_tb),
        in_specs=[
            pl.BlockSpec((1, seq_hi, t_lo_per_step, D_MODEL), lambda i, t: (i, 0, t, 0)),
            _layer_spec(layer, (V7X_SUBLANES, D_MODEL)),
            _layer_spec(layer, (D_MODEL, GROUP), (0, W_F_COL // GROUP)),
            _const_spec((2 * GROUP, 2 * GROUP)),
            _const_spec((2 * rows, rows)),
            pl.BlockSpec((FOURIER_SUB, rows, V7X_LANES), lambda i, t: (t, 0, 0)),
            pl.BlockSpec((FOURIER_SUB, rows, V7X_LANES), lambda i, t: (t, 0, 0)),
            _const_spec((SEQ_LO, 2 * SEQ_LO)),
        ],
        out_specs=pl.BlockSpec(out_block, lambda i, t: (0, i, 0, 0, 0, 0)),
        out_shape=jax.ShapeDtypeStruct((halves, b) + out_block[2:], F32),
        scratch_shapes=[
            pltpu.VMEM((2 * SEQ_LO, seq_hi * GROUP), BF16),
            pltpu.VMEM((halves, SEQ_LO * V7X_SUBLANES, V7X_LANES), F32),
        ],
        compiler_params=pltpu.CompilerParams(
            dimension_semantics=("arbitrary", "arbitrary"),
            vmem_limit_bytes=V7X_VMEM_LIMIT_BYTES),
        name="fourier_spec",
    )(x4, params["gains"], params["w_rows"], fc, kd, twc, tws, e)
    return out.reshape(halves, b * seq, V7X_LANES)


def _mixer_ffn_kernel(x_ref, xp_ref, xn_ref, spec_ref, gains_ref, vec_ref, mats_ref, wcat_ref, w_rows_ref,
                      wd_ref,
                      o_ref, cz_s, p_s, ps_s, act_s, x1_s, h2_s, *, tile, seq, n_tiles):
    g_pre_ref, g_post_ref, g_pre_ffn_ref, g_post_ffn_ref, g_grp_ref = (
        gains_ref.at[k:k + 1] for k in range(5))
    sbias_ref = vec_ref.at[0:CHUNK]
    conv_w_ref = vec_ref.at[CHUNK:CHUNK + 3]
    pool_scale_ref = vec_ref.at[CHUNK + 3:CHUNK + 4]
    pool_w_ref, four_w_ref, mavg_ref = (mats_ref.at[k] for k in range(3))
    w_in_ref = w_rows_ref.at[:, W_IN_COL:W_IN_COL + 6 * GROUP]
    w_out_ref = w_rows_ref.at[:, W_OUT_COL:W_OUT_COL + D_MODEL]
    wg_ref = w_rows_ref.at[:, W_GATE_COL:W_GATE_COL + D_FF]
    wu_ref = w_rows_ref.at[:, W_UP_COL:W_UP_COL + D_FF]
    step = pl.program_id(0)

    @pl.when(step == 0)
    def _():
        x1_s[...] = jnp.zeros_like(x1_s)
        h2_s[...] = jnp.zeros_like(h2_s)

    def ffn_chunks(lo, hi):
        for j in range(lo, hi):
            ls = slice(j * FF_CHUNK, (j + 1) * FF_CHUNK)
            gate = _dot(h2_s[...], wg_ref[:, ls])
            up = _dot(h2_s[...], wu_ref[:, ls])
            act_s[:, ls] = (gate * jax.nn.sigmoid(gate) * up).astype(BF16)

    def ffn_down(rows):
        f = _dot(act_s[rows, :], wd_ref[...])
        o_ref[rows, :] = o_ref[rows, :] + _rms(f, g_post_ffn_ref[...])

    o_ref[...] = x1_s[...]
    ffn_chunks(0, 1)

    tiles_per_seq = seq // tile
    ti = jnp.minimum(step, n_tiles - 1) % tiles_per_seq
    x = x_ref[...]
    xh = jnp.concatenate([xp_ref[...], xn_ref[...]], axis=0)

    hm = _rms(x, g_pre_ref[...]).astype(BF16)
    hh = _rms(xh, g_pre_ref[...]).astype(BF16)
    ffn_chunks(1, 2)
    pe = _dot(jnp.concatenate([hm, hh], axis=0), w_in_ref[:, :3 * GROUP])
    pm = jnp.concatenate([pe[:tile], _dot(hm, w_in_ref[:, 3 * GROUP:])], axis=1)
    ph = pe[tile:]
    ffn_chunks(2, 3)
    hrow = lax.broadcasted_iota(jnp.int32, (2 * HALO, 1), 0)
    has_prev = (ti != 0).astype(jnp.int32)
    has_next = (ti != tiles_per_seq - 1).astype(jnp.int32)
    ph = jnp.where(jnp.where(hrow < HALO, has_prev, has_next) != 0, ph, 0.0)

    cz_s[pl.ds(0, HALO), :] = ph[:HALO, :GROUP] * ph[:HALO, GROUP:2 * GROUP]
    cz_s[pl.ds(HALO, tile), :] = pm[:, :GROUP] * pm[:, GROUP:2 * GROUP]
    cz_s[pl.ds(HALO + tile, HALO), :] = ph[HALO:, :GROUP] * ph[HALO:, GROUP:2 * GROUP]
    p_s[pl.ds(0, HALO), :] = ph[:HALO, 2 * GROUP:]
    p_s[pl.ds(HALO, tile), :] = pm[:, 2 * GROUP:3 * GROUP]
    p_s[pl.ds(HALO + tile, HALO), :] = ph[HALO:, 2 * GROUP:]
    p_s[pl.ds(2 * HALO + tile, V7X_SUBLANES), :] = jnp.zeros((V7X_SUBLANES, GROUP), F32)
    b_gate = pm[:, 3 * GROUP:4 * GROUP]
    u = pm[:, 4 * GROUP:5 * GROUP]
    v = pm[:, 5 * GROUP:]

    y_conv = b_gate * (conv_w_ref[0:1, :] * cz_s[pl.ds(HALO - 1, tile), :]
                       + conv_w_ref[1:2, :] * cz_s[pl.ds(HALO, tile), :]
                       + conv_w_ref[2:3, :] * cz_s[pl.ds(HALO + 1, tile), :])
    ffn_chunks(3, 4)

    pos = ti * tile + lax.broadcasted_iota(jnp.int32, (tile, 1), 0)
    lane = lax.broadcasted_iota(jnp.int32, (1, V7X_LANES), 1)
    first_half = lane < HEAD_DIM
    def pool_col(col, inner, outer):
        w_small, w_big = POOL_WINDOWS[2 * col], POOL_WINDOWS[2 * col + 1]
        half = jnp.where(first_half, w_small // 2, w_big // 2)
        cnt = (jnp.minimum(pos + half, seq) - jnp.maximum(pos - half, 0)).astype(F32)
        mean = jnp.where(first_half, inner, outer) / cnt
        return mean - p_s[pl.ds(HALO, tile), col * V7X_LANES:(col + 1) * V7X_LANES]

    ls0 = slice(0, V7X_LANES)
    inner0 = p_s[pl.ds(HALO - 1, tile), ls0] + p_s[pl.ds(HALO, tile), ls0]
    outer0 = inner0 + p_s[pl.ds(HALO - 2, tile), ls0] + p_s[pl.ds(HALO + 1, tile), ls0]
    d_col0 = pool_col(0, inner0, outer0)
    ffn_chunks(4, 5)
    ls1 = slice(V7X_LANES, 2 * V7X_LANES)
    base = HALO - POOL_WINDOWS[3] // 2
    n2, n4, n8 = tile + 24, tile + 16, tile + 8
    ps_s[0, pl.ds(base, n2), :] = p_s[pl.ds(base, n2), ls1] + p_s[pl.ds(base + 1, n2), ls1]
    ps_s[1, pl.ds(base, n4), :] = ps_s[0, pl.ds(base, n4), :] + ps_s[0, pl.ds(base + 2, n4), :]
    ps_s[0, pl.ds(base, n8), :] = ps_s[1, pl.ds(base, n8), :] + ps_s[1, pl.ds(base + 4, n8), :]
    inner1 = ps_s[0, pl.ds(HALO - 4, tile), :]
    outer1 = ps_s[0, pl.ds(HALO - 8, tile), :] + ps_s[0, pl.ds(HALO, tile), :]
    d = jnp.concatenate([d_col0, pool_col(1, inner1, outer1)], axis=1).astype(BF16)
    y_pool = _dot(d, pool_w_ref[...]) * pool_scale_ref[...]

    spec = jnp.concatenate([spec_ref[k] for k in range(GROUP // V7X_LANES)], axis=1).astype(BF16)
    y_four = _dot(spec, four_w_ref[...])
    ffn_chunks(5, 6)

    mu = _dot(v.astype(BF16), mavg_ref[...])
    vc = v - mu
    var = _dot((vc * vc).astype(BF16), mavg_ref[...])
    vh = vc * lax.rsqrt(var + EPS)
    ffn_chunks(6, 7)
    lane_g = lax.broadcasted_iota(jnp.int32, (1, GROUP), 1)
    s_chunks = []
    for n in range(tile // CHUNK):
        vn = vh[n * CHUNK:(n + 1) * CHUNK]
        stack = jnp.concatenate(
            [jnp.where(lane_g // HEAD_DIM == hd, vn, 0.0).astype(BF16) for hd in range(N_HEADS)], axis=0)
        s_chunks.append(_dot(wcat_ref[...], stack) + sbias_ref[...])
    y_gmlp = u * jnp.concatenate(s_chunks, axis=0)
    ffn_chunks(7, 8)

    y = jnp.concatenate(
        [_rms(yk, g_grp_ref[:, k * GROUP:(k + 1) * GROUP]).astype(BF16)
         for k, yk in enumerate((y_conv, y_pool, y_four, y_gmlp))], axis=1)
    ffn_chunks(8, 9)

    def out_half(rows):
        m = _dot(y[rows, :], w_out_ref[...])
        x1 = x_ref[rows, :] + _rms(m, g_post_ref[...])
        return x1, _rms(x1, g_pre_ffn_ref[...]).astype(BF16)

    top, bot = slice(0, tile // 2), slice(tile // 2, tile)
    x1_top, h2_top = out_half(top)
    ffn_chunks(9, 10)
    x1_bot, h2_bot = out_half(bot)
    ffn_chunks(10, D_FF // FF_CHUNK)
    ffn_down(top)
    x1_s[top, :] = x1_top
    h2_s[top, :] = h2_top
    x1_s[bot, :] = x1_bot
    h2_s[bot, :] = h2_bot
    ffn_down(bot)


def _mixer_ffn(x2, spec, seq, layer, p):
    n_tok = x2.shape[0]
    tile = min(TOKEN_TILE, seq)
    n_tiles = n_tok // tile
    n_halo_blocks = n_tok // HALO
    per_tile = tile // HALO
    cur = lambda i: jnp.minimum(i, n_tiles - 1)
    return pl.pallas_call(
        functools.partial(_mixer_ffn_kernel, tile=tile, seq=seq, n_tiles=n_tiles),
        grid=(n_tiles + 1,),
        in_specs=[
            pl.BlockSpec((tile, D_MODEL), lambda i: (cur(i), 0)),
            pl.BlockSpec((HALO, D_MODEL), lambda i: (jnp.maximum(cur(i) * per_tile - 1, 0), 0)),
            pl.BlockSpec((HALO, D_MODEL),
                         lambda i: (jnp.minimum((cur(i) + 1) * per_tile, n_halo_blocks - 1), 0)),
            pl.BlockSpec((GROUP // V7X_LANES, tile, V7X_LANES), lambda i: (0, cur(i), 0)),
            _layer_spec(layer, p["gains"].shape[1:]),
            _layer_spec(layer, p["vec"].shape[1:]),
            _layer_spec(layer, p["mats"].shape[1:]),
            _layer_spec(layer, p["wcat"].shape[1:]),
            _layer_spec(layer, p["w_rows"].shape[1:]),
            _layer_spec(layer, p["wd"].shape[1:]),
        ],
        out_specs=pl.BlockSpec((tile, D_MODEL), lambda i: (jnp.maximum(i - 1, 0), 0)),
        out_shape=jax.ShapeDtypeStruct(x2.shape, x2.dtype),
        scratch_shapes=[
            pltpu.VMEM((tile + 2 * HALO, GROUP), F32),
            pltpu.VMEM((tile + 2 * HALO + V7X_SUBLANES, GROUP), F32),
            pltpu.VMEM((2, tile + 2 * HALO, V7X_LANES), F32),
            pltpu.VMEM((tile, D_FF), BF16),
            pltpu.VMEM((tile, D_MODEL), F32),
            pltpu.VMEM((tile, D_MODEL), BF16),
        ],
        compiler_params=pltpu.CompilerParams(
            dimension_semantics=("arbitrary",),
            vmem_limit_bytes=V7X_VMEM_LIMIT_BYTES),
        name="mixer_ffn",
    )(x2, x2, x2, spec, p["gains"], p["vec"], p["mats"], p["wcat"], p["w_rows"], p["wd"])


def _block_diag(blocks):
    depth, n, k, _ = blocks.shape
    on_diagonal = jnp.eye(n, dtype=bool)[None, :, None, :, None]
    return jnp.where(on_diagonal, blocks[:, :, :, None, :], 0.0).reshape(depth, n * k, n * k)


def _pack_params(pre_mix_gain, post_mix_gain, pre_ffn_gain, post_ffn_gain, w_in, conv_w, pool_w, pool_scale,
                 fourier_w, spatial_w, spatial_b, group_norm_gain, w_out, w_gate, w_up, w_down):
    depth = w_in.shape[0]
    sl = lambda k: w_in[:, :, k * GROUP:(k + 1) * GROUP]
    w_rows = jnp.concatenate([sl(1), sl(2), sl(3), sl(0), sl(5), sl(6), sl(4), w_out, w_gate, w_up],
                             axis=2).astype(BF16)
    assert w_rows.shape[2] == W_ROWS_COLS
    mavg = np.kron(np.eye(N_HEADS), np.full((HEAD_DIM, HEAD_DIM), 1.0 / HEAD_DIM))
    mavg = jnp.broadcast_to(jnp.asarray(mavg, F32), (depth, GROUP, GROUP))
    sbias = jnp.repeat(jnp.transpose(spatial_b, (0, 2, 1)), HEAD_DIM, axis=2)
    row = lambda g: g[:, None, :]
    return dict(
        gains=jnp.concatenate([row(pre_mix_gain), row(post_mix_gain), row(pre_ffn_gain), row(post_ffn_gain),
                               row(group_norm_gain), jnp.zeros((depth, V7X_SUBLANES - 5, D_MODEL), F32)],
                              axis=1),
        vec=jnp.concatenate([sbias, conv_w, row(pool_scale),
                             jnp.zeros((depth, V7X_SUBLANES - 4, GROUP), F32)], axis=1),
        mats=jnp.stack([_block_diag(pool_w), _block_diag(fourier_w), mavg], axis=1).astype(BF16),
        wcat=jnp.transpose(spatial_w, (0, 2, 1, 3)).reshape(depth, CHUNK, N_HEADS * CHUNK).astype(BF16),
        w_rows=w_rows,
        wd=w_down.astype(BF16),
    )


def kernel(x, pre_mix_gain, post_mix_gain, pre_ffn_gain, post_ffn_gain, w_in, conv_w, pool_w, pool_scale,
           fourier_w, spatial_w, spatial_b, group_norm_gain, w_out, w_gate, w_up, w_down):
    b, seq, d = x.shape
    assert d == D_MODEL and seq % SEQ_LO == 0 and seq % min(TOKEN_TILE, seq) == 0
    tables = _fourier_tables(seq)
    p = _pack_params(pre_mix_gain, post_mix_gain, pre_ffn_gain, post_ffn_gain, w_in, conv_w, pool_w,
                     pool_scale, fourier_w, spatial_w, spatial_b, group_norm_gain, w_out, w_gate, w_up,
                     w_down)
    for layer in range(w_in.shape[0]):
        spec = _fourier_spec(x, layer, p, tables)
        x = _mixer_ffn(x.reshape(b * seq, d), spec, seq, layer, p).reshape(b, seq, d)
    return x
```

```python
import functools

import jax
import jax.numpy as jnp
import numpy as np
from jax import lax
from jax.experimental import pallas as pl
from jax.experimental.pallas import tpu as pltpu

D_MODEL = 1024
GROUP = 256
HEAD_DIM = 64
N_HEADS = GROUP // HEAD_DIM
CHUNK = 128
D_FF = 2816
POOL_WINDOWS = (2, 4, 8, 16)
EPS = 1e-6

V7X_LANES = 128
V7X_SUBLANES = 8
V7X_MXU_COLS = 256
V7X_VMEM_LIMIT_BYTES = 56 * 1024 * 1024

SEQ_LO = 128
HALO = 16
TOKEN_TILE = 512
FF_CHUNK = V7X_MXU_COLS
W_IN_COL = 0
W_F_COL = W_IN_COL + 6 * GROUP
W_OUT_COL = W_F_COL + GROUP
W_GATE_COL = W_OUT_COL + D_MODEL
W_UP_COL = W_GATE_COL + D_FF
W_ROWS_COLS = W_UP_COL + D_FF
FOURIER_SUB = 4

F32 = jnp.float32
BF16 = jnp.bfloat16


def _dot(a, b):
    return jnp.dot(a, b, preferred_element_type=F32)


def _rms(x, g):
    ms = jnp.mean(x * x, axis=-1, keepdims=True)
    return x * lax.rsqrt(ms + EPS) * g


def _fourier_tables(seq):
    seq_hi = seq // SEQ_LO
    cm = np.outer(np.arange(HEAD_DIM), np.arange(HEAD_DIM)) % HEAD_DIM
    ang = 2.0 * np.pi * cm / HEAD_DIM
    cos_bd = np.kron(np.eye(N_HEADS), np.cos(ang)) / np.sqrt(HEAD_DIM)
    sin_bd = np.kron(np.eye(N_HEADS), np.sin(ang)) / np.sqrt(HEAD_DIM)
    fc = np.block([[cos_bd, -sin_bd], [sin_bd, cos_bd]])
    kt = np.outer(np.arange(seq_hi), np.arange(seq_hi)) % seq_hi
    ang1 = 2.0 * np.pi * kt / seq_hi
    eye = np.eye(V7X_SUBLANES)
    c1 = np.kron(np.cos(ang1) / np.sqrt(seq_hi), eye)
    s1 = np.kron(np.sin(ang1) / np.sqrt(seq_hi), eye)
    kd = np.concatenate([c1, -s1], axis=0)
    k = np.arange(SEQ_LO)[None, :, None] * seq_hi + np.arange(seq_hi)[:, None, None]
    ang2 = 2.0 * np.pi * ((k * np.arange(SEQ_LO)[None, None, :]) % seq) / seq
    e = np.concatenate([np.cos(ang2), np.sin(ang2)], axis=2) / np.sqrt(SEQ_LO)
    return (jnp.asarray(fc, F32).astype(BF16), jnp.asarray(kd, F32).astype(BF16),
            jnp.asarray(e, F32).astype(BF16))


def _fourier_kernel(x_ref, g_ref, wf_ref, fc_ref, kd_ref, e_ref, o_ref, y_s, il_s, f_s, *, seq_hi, n_tb):
    step = pl.program_id(0)
    tb = jnp.maximum(step - 1, 0) % n_tb
    rows = seq_hi * V7X_SUBLANES
    t_lo_per_step = V7X_SUBLANES * FOURIER_SUB

    @pl.when(step == 0)
    def _():
        f_s[...] = jnp.zeros_like(f_s)

    def stage_norm(q):
        x = x_ref[0, :, q * V7X_SUBLANES:(q + 1) * V7X_SUBLANES, :].reshape(rows, D_MODEL)
        inv = lax.rsqrt(jnp.mean(x * x, axis=-1, keepdims=True) + EPS)
        return (x * g_ref[0:1, :]).astype(BF16), inv

    def stage_proj(q, h_inv):
        h, inv = h_inv
        return (_dot(h, wf_ref[...]) * inv).astype(BF16)

    def stage_slow_dft(q, f):
        u = _dot(kd_ref[...], f)
        return jnp.concatenate([u[:rows], u[rows:]], axis=1).astype(BF16)

    def stage_channel_dft(q, uc):
        return _dot(uc, fc_ref[...])

    vals = []
    for q in range(FOURIER_SUB):
        h_inv = stage_norm(q)
        uc = stage_slow_dft(q, f_s[q])
        f_s[q] = stage_proj(q, h_inv)
        vals.append(stage_channel_dft(q, uc))

    row = pl.multiple_of(tb * t_lo_per_step, t_lo_per_step)
    for k in range(seq_hi):
        rs = slice(k * V7X_SUBLANES, (k + 1) * V7X_SUBLANES)
        ls = slice(k * GROUP, (k + 1) * GROUP)
        piece = jnp.concatenate([v[rs] for v in vals], axis=0).astype(BF16)
        y_s[pl.ds(row, t_lo_per_step), ls] = piece[:, :GROUP]
        y_s[pl.ds(SEQ_LO + row, t_lo_per_step), ls] = piece[:, GROUP:]

    @pl.when((step > 0) & (tb == n_tb - 1))
    def _():
        def out_block(kb, carry):
            for j in range(V7X_SUBLANES):
                k_lo = kb * V7X_SUBLANES + j
                res = _dot(e_ref[k_lo], y_s[:, pl.ds(pl.multiple_of(k_lo * GROUP, GROUP), GROUP)])
                for half in range(GROUP // V7X_LANES):
                    ls = slice(half * V7X_LANES, (half + 1) * V7X_LANES)
                    il_s[half, pl.ds(j, SEQ_LO, stride=V7X_SUBLANES), :] = res[:, ls]
            for half in range(GROUP // V7X_LANES):
                o_ref[half, 0, :, kb, :, :] = il_s[half].reshape(SEQ_LO, V7X_SUBLANES, V7X_LANES)
            return carry

        lax.fori_loop(0, seq_hi // V7X_SUBLANES, out_block, 0)


def _const_spec(shape):
    n = len(shape)
    return pl.BlockSpec(shape, lambda *_: (0,) * n, pipeline_mode=pl.Buffered(1))


def _layer_spec(layer, shape, block_index=None):
    index = (0,) * len(shape) if block_index is None else block_index
    return pl.BlockSpec((None,) + shape, lambda *_: (layer,) + index, pipeline_mode=pl.Buffered(1))


def _fourier_spec(x, layer, params, tables):
    b, seq, _ = x.shape
    seq_hi = seq // SEQ_LO
    t_lo_per_step = V7X_SUBLANES * FOURIER_SUB
    n_tb = SEQ_LO // t_lo_per_step
    rows = seq_hi * V7X_SUBLANES
    fc, kd, e = tables
    halves = GROUP // V7X_LANES
    k_lo_blocks = seq_hi // V7X_SUBLANES
    out_block = (halves, 1, SEQ_LO, k_lo_blocks, V7X_SUBLANES, V7X_LANES)
    x4 = x.reshape(b, seq_hi, SEQ_LO, D_MODEL)
    n_blocks = b * n_tb
    proj_block = lambda s: jnp.minimum(s, n_blocks - 1)
    dft_batch = lambda s: jnp.maximum(s - 1, 0) // n_tb
    out = pl.pallas_call(
        functools.partial(_fourier_kernel, seq_hi=seq_hi, n_tb=n_tb),
        grid=(n_blocks + 1,),
        in_specs=[
            pl.BlockSpec((1, seq_hi, t_lo_per_step, D_MODEL),
                         lambda s: (proj_block(s) // n_tb, 0, proj_block(s) % n_tb, 0)),
            _layer_spec(layer, (V7X_SUBLANES, D_MODEL)),
            _layer_spec(layer, (D_MODEL, GROUP), (0, W_F_COL // GROUP)),
            _const_spec((2 * GROUP, 2 * GROUP)),
            _const_spec((2 * rows, rows)),
            _const_spec((seq_hi, SEQ_LO, 2 * SEQ_LO)),
        ],
        out_specs=pl.BlockSpec(out_block, lambda s: (0, dft_batch(s), 0, 0, 0, 0)),
        out_shape=jax.ShapeDtypeStruct((halves, b) + out_block[2:], F32),
        scratch_shapes=[
            pltpu.VMEM((2 * SEQ_LO, seq_hi * GROUP), BF16),
            pltpu.VMEM((halves, SEQ_LO * V7X_SUBLANES, V7X_LANES), F32),
            pltpu.VMEM((FOURIER_SUB, rows, GROUP), BF16),
        ],
        compiler_params=pltpu.CompilerParams(
            dimension_semantics=("arbitrary",),
            vmem_limit_bytes=V7X_VMEM_LIMIT_BYTES),
        name="fourier_spec",
    )(x4, params["gains"], params["w_rows"], fc, kd, e)
    return out.reshape(halves, b * seq, V7X_LANES)


def _mixer_ffn_kernel(x_ref, xp_ref, xn_ref, spec_ref, gains_ref, vec_ref, mats_ref, wcat_ref, w_rows_ref,
                      wd_ref,
                      o_ref, cz_s, p_s, ps_s, act_s, x1_s, h2_s, *, tile, seq, n_tiles):
    g_pre_ref, g_post_ref, g_pre_ffn_ref, g_post_ffn_ref, g_grp_ref = (
        gains_ref.at[k:k + 1] for k in range(5))
    sbias_ref = vec_ref.at[0:CHUNK]
    conv_w_ref = vec_ref.at[CHUNK:CHUNK + 3]
    pool_scale_ref = vec_ref.at[CHUNK + 3:CHUNK + 4]
    pool_w_ref, four_w_ref, mavg_ref = (mats_ref.at[k] for k in range(3))
    w_in_ref = w_rows_ref.at[:, W_IN_COL:W_IN_COL + 6 * GROUP]
    w_out_ref = w_rows_ref.at[:, W_OUT_COL:W_OUT_COL + D_MODEL]
    wg_ref = w_rows_ref.at[:, W_GATE_COL:W_GATE_COL + D_FF]
    wu_ref = w_rows_ref.at[:, W_UP_COL:W_UP_COL + D_FF]
    step = pl.program_id(0)

    @pl.when(step == 0)
    def _():
        x1_s[...] = jnp.zeros_like(x1_s)
        h2_s[...] = jnp.zeros_like(h2_s)

    def ffn_chunks(lo, hi):
        for j in range(lo, hi):
            ls = slice(j * FF_CHUNK, (j + 1) * FF_CHUNK)
            gate = _dot(h2_s[...], wg_ref[:, ls])
            up = _dot(h2_s[...], wu_ref[:, ls])
            act_s[:, ls] = (gate * jax.nn.sigmoid(gate) * up).astype(BF16)

    def ffn_down(rows):
        f = _dot(act_s[rows, :], wd_ref[...])
        o_ref[rows, :] = o_ref[rows, :] + _rms(f, g_post_ffn_ref[...])

    o_ref[...] = x1_s[...]
    ffn_chunks(0, 1)

    tiles_per_seq = seq // tile
    ti = jnp.minimum(step, n_tiles - 1) % tiles_per_seq
    x = x_ref[...]
    xh = jnp.concatenate([xp_ref[...], xn_ref[...]], axis=0)

    hm = _rms(x, g_pre_ref[...]).astype(BF16)
    hh = _rms(xh, g_pre_ref[...]).astype(BF16)
    ffn_chunks(1, 2)
    pe = _dot(jnp.concatenate([hm, hh], axis=0), w_in_ref[:, :3 * GROUP])
    pm = jnp.concatenate([pe[:tile], _dot(hm, w_in_ref[:, 3 * GROUP:])], axis=1)
    ph = pe[tile:]
    ffn_chunks(2, 3)
    hrow = lax.broadcasted_iota(jnp.int32, (2 * HALO, 1), 0)
    has_prev = (ti != 0).astype(jnp.int32)
    has_next = (ti != tiles_per_seq - 1).astype(jnp.int32)
    ph = jnp.where(jnp.where(hrow < HALO, has_prev, has_next) != 0, ph, 0.0)

    cz_s[pl.ds(0, HALO), :] = ph[:HALO, :GROUP] * ph[:HALO, GROUP:2 * GROUP]
    cz_s[pl.ds(HALO, tile), :] = pm[:, :GROUP] * pm[:, GROUP:2 * GROUP]
    cz_s[pl.ds(HALO + tile, HALO), :] = ph[HALO:, :GROUP] * ph[HALO:, GROUP:2 * GROUP]
    p_s[pl.ds(0, HALO), :] = ph[:HALO, 2 * GROUP:]
    p_s[pl.ds(HALO, tile), :] = pm[:, 2 * GROUP:3 * GROUP]
    p_s[pl.ds(HALO + tile, HALO), :] = ph[HALO:, 2 * GROUP:]
    p_s[pl.ds(2 * HALO + tile, V7X_SUBLANES), :] = jnp.zeros((V7X_SUBLANES, GROUP), F32)
    b_gate = pm[:, 3 * GROUP:4 * GROUP]
    u = pm[:, 4 * GROUP:5 * GROUP]
    v = pm[:, 5 * GROUP:]

    y_conv = b_gate * (conv_w_ref[0:1, :] * cz_s[pl.ds(HALO - 1, tile), :]
                       + conv_w_ref[1:2, :] * cz_s[pl.ds(HALO, tile), :]
                       + conv_w_ref[2:3, :] * cz_s[pl.ds(HALO + 1, tile), :])
    ffn_chunks(3, 4)

    pos = ti * tile + lax.broadcasted_iota(jnp.int32, (tile, 1), 0)
    lane = lax.broadcasted_iota(jnp.int32, (1, V7X_LANES), 1)
    first_half = lane < HEAD_DIM
    def pool_col(col, inner, outer):
        w_small, w_big = POOL_WINDOWS[2 * col], POOL_WINDOWS[2 * col + 1]
        half = jnp.where(first_half, w_small // 2, w_big // 2)
        cnt = (jnp.minimum(pos + half, seq) - jnp.maximum(pos - half, 0)).astype(F32)
        mean = jnp.where(first_half, inner, outer) / cnt
        return mean - p_s[pl.ds(HALO, tile), col * V7X_LANES:(col + 1) * V7X_LANES]

    ls0 = slice(0, V7X_LANES)
    inner0 = p_s[pl.ds(HALO - 1, tile), ls0] + p_s[pl.ds(HALO, tile), ls0]
    outer0 = inner0 + p_s[pl.ds(HALO - 2, tile), ls0] + p_s[pl.ds(HALO + 1, tile), ls0]
    d_col0 = pool_col(0, inner0, outer0)
    ffn_chunks(4, 5)
    ls1 = slice(V7X_LANES, 2 * V7X_LANES)
    base = HALO - POOL_WINDOWS[3] // 2
    n2, n4, n8 = tile + 24, tile + 16, tile + 8
    ps_s[0, pl.ds(base, n2), :] = p_s[pl.ds(base, n2), ls1] + p_s[pl.ds(base + 1, n2), ls1]
    ps_s[1, pl.ds(base, n4), :] = ps_s[0, pl.ds(base, n4), :] + ps_s[0, pl.ds(base + 2, n4), :]
    ps_s[0, pl.ds(base, n8), :] = ps_s[1, pl.ds(base, n8), :] + ps_s[1, pl.ds(base + 4, n8), :]
    inner1 = ps_s[0, pl.ds(HALO - 4, tile), :]
    outer1 = ps_s[0, pl.ds(HALO - 8, tile), :] + ps_s[0, pl.ds(HALO, tile), :]
    d = jnp.concatenate([d_col0, pool_col(1, inner1, outer1)], axis=1).astype(BF16)
    y_pool = _dot(d, pool_w_ref[...]) * pool_scale_ref[...]

    spec = jnp.concatenate([spec_ref[k] for k in range(GROUP // V7X_LANES)], axis=1).astype(BF16)
    y_four = _dot(spec, four_w_ref[...])
    ffn_chunks(5, 6)

    mu = _dot(v.astype(BF16), mavg_ref[...])
    vc = v - mu
    var = _dot((vc * vc).astype(BF16), mavg_ref[...])
    vh = vc * lax.rsqrt(var + EPS)
    ffn_chunks(6, 7)
    lane_g = lax.broadcasted_iota(jnp.int32, (1, GROUP), 1)
    s_chunks = []
    for n in range(tile // CHUNK):
        vn = vh[n * CHUNK:(n + 1) * CHUNK]
        stack = jnp.concatenate(
            [jnp.where(lane_g // HEAD_DIM == hd, vn, 0.0).astype(BF16) for hd in range(N_HEADS)], axis=0)
        s_chunks.append(_dot(wcat_ref[...], stack) + sbias_ref[...])
    y_gmlp = u * jnp.concatenate(s_chunks, axis=0)
    ffn_chunks(7, 8)

    y = jnp.concatenate(
        [_rms(yk, g_grp_ref[:, k * GROUP:(k + 1) * GROUP]).astype(BF16)
         for k, yk in enumerate((y_conv, y_pool, y_four, y_gmlp))], axis=1)
    ffn_chunks(8, 9)

    def out_half(rows):
        m = _dot(y[rows, :], w_out_ref[...])
        x1 = x_ref[rows, :] + _rms(m, g_post_ref[...])
        return x1, _rms(x1, g_pre_ffn_ref[...]).astype(BF16)

    top, bot = slice(0, tile // 2), slice(tile // 2, tile)
    x1_top, h2_top = out_half(top)
    ffn_chunks(9, 10)
    x1_bot, h2_bot = out_half(bot)
    ffn_chunks(10, D_FF // FF_CHUNK)
    ffn_down(top)
    x1_s[top, :] = x1_top
    h2_s[top, :] = h2_top
    x1_s[bot, :] = x1_bot
    h2_s[bot, :] = h2_bot
    ffn_down(bot)


def _mixer_ffn(x2, spec, seq, layer, p):
    n_tok = x2.shape[0]
    tile = min(TOKEN_TILE, seq)
    n_tiles = n_tok // tile
    n_halo_blocks = n_tok // HALO
    per_tile = tile // HALO
    cur = lambda i: jnp.minimum(i, n_tiles - 1)
    return pl.pallas_call(
        functools.partial(_mixer_ffn_kernel, tile=tile, seq=seq, n_tiles=n_tiles),
        grid=(n_tiles + 1,),
        in_specs=[
            pl.BlockSpec((tile, D_MODEL), lambda i: (cur(i), 0)),
            pl.BlockSpec((HALO, D_MODEL), lambda i: (jnp.maximum(cur(i) * per_tile - 1, 0), 0)),
            pl.BlockSpec((HALO, D_MODEL),
                         lambda i: (jnp.minimum((cur(i) + 1) * per_tile, n_halo_blocks - 1), 0)),
            pl.BlockSpec((GROUP // V7X_LANES, tile, V7X_LANES), lambda i: (0, cur(i), 0)),
            _layer_spec(layer, p["gains"].shape[1:]),
            _layer_spec(layer, p["vec"].shape[1:]),
            _layer_spec(layer, p["mats"].shape[1:]),
            _layer_spec(layer, p["wcat"].shape[1:]),
            _layer_spec(layer, p["w_rows"].shape[1:]),
            _layer_spec(layer, p["wd"].shape[1:]),
        ],
        out_specs=pl.BlockSpec((tile, D_MODEL), lambda i: (jnp.maximum(i - 1, 0), 0)),
        out_shape=jax.ShapeDtypeStruct(x2.shape, x2.dtype),
        scratch_shapes=[
            pltpu.VMEM((tile + 2 * HALO, GROUP), F32),
            pltpu.VMEM((tile + 2 * HALO + V7X_SUBLANES, GROUP), F32),
            pltpu.VMEM((2, tile + 2 * HALO, V7X_LANES), F32),
            pltpu.VMEM((tile, D_FF), BF16),
            pltpu.VMEM((tile, D_MODEL), F32),
            pltpu.VMEM((tile, D_MODEL), BF16),
        ],
        compiler_params=pltpu.CompilerParams(
            dimension_semantics=("arbitrary",),
            vmem_limit_bytes=V7X_VMEM_LIMIT_BYTES),
        name="mixer_ffn",
    )(x2, x2, x2, spec, p["gains"], p["vec"], p["mats"], p["wcat"], p["w_rows"], p["wd"])


def _block_diag(blocks):
    depth, n, k, _ = blocks.shape
    on_diagonal = jnp.eye(n, dtype=bool)[None, :, None, :, None]
    return jnp.where(on_diagonal, blocks[:, :, :, None, :], 0.0).reshape(depth, n * k, n * k)


def _pack_params(pre_mix_gain, post_mix_gain, pre_ffn_gain, post_ffn_gain, w_in, conv_w, pool_w, pool_scale,
                 fourier_w, spatial_w, spatial_b, group_norm_gain, w_out, w_gate, w_up, w_down):
    depth = w_in.shape[0]
    sl = lambda k: w_in[:, :, k * GROUP:(k + 1) * GROUP]
    w_rows = jnp.concatenate([sl(1), sl(2), sl(3), sl(0), sl(5), sl(6), sl(4), w_out, w_gate, w_up],
                             axis=2).astype(BF16)
    assert w_rows.shape[2] == W_ROWS_COLS
    mavg = np.kron(np.eye(N_HEADS), np.full((HEAD_DIM, HEAD_DIM), 1.0 / HEAD_DIM))
    mavg = jnp.broadcast_to(jnp.asarray(mavg, F32), (depth, GROUP, GROUP))
    sbias = jnp.repeat(jnp.transpose(spatial_b, (0, 2, 1)), HEAD_DIM, axis=2)
    row = lambda g: g[:, None, :]
    return dict(
        gains=jnp.concatenate([row(pre_mix_gain), row(post_mix_gain), row(pre_ffn_gain), row(post_ffn_gain),
                               row(group_norm_gain), jnp.zeros((depth, V7X_SUBLANES - 5, D_MODEL), F32)],
                              axis=1),
        vec=jnp.concatenate([sbias, conv_w, row(pool_scale),
                             jnp.zeros((depth, V7X_SUBLANES - 4, GROUP), F32)], axis=1),
        mats=jnp.stack([_block_diag(pool_w), _block_diag(fourier_w), mavg], axis=1).astype(BF16),
        wcat=jnp.transpose(spatial_w, (0, 2, 1, 3)).reshape(depth, CHUNK, N_HEADS * CHUNK).astype(BF16),
        w_rows=w_rows,
        wd=w_down.astype(BF16),
    )


def kernel(x, pre_mix_gain, post_mix_gain, pre_ffn_gain, post_ffn_gain, w_in, conv_w, pool_w, pool_scale,
           fourier_w, spatial_w, spatial_b, group_norm_gain, w_out, w_gate, w_up, w_down):
    b, seq, d = x.shape
    assert d == D_MODEL and seq % SEQ_LO == 0 and seq % min(TOKEN_TILE, seq) == 0
    tables = _fourier_tables(seq)
    p = _pack_params(pre_mix_gain, post_mix_gain, pre_ffn_gain, post_ffn_gain, w_in, conv_w, pool_w,
                     pool_scale, fourier_w, spatial_w, spatial_b, group_norm_gain, w_out, w_gate, w_up,
                     w_down)
    for layer in range(w_in.shape[0]):
        spec = _fourier_spec(x, layer, p, tables)
        x = _mixer_ffn(x.reshape(b * seq, d), spec, seq, layer, p).reshape(b, seq, d)
    return x
```

```python
import functools

import jax
import jax.numpy as jnp
import numpy as np
from jax import lax
from jax.experimental import pallas as pl
from jax.experimental.pallas import tpu as pltpu

D_MODEL = 1024
GROUP = 256
HEAD_DIM = 64
N_HEADS = GROUP // HEAD_DIM
CHUNK = 128
D_FF = 2816
POOL_WINDOWS = (2, 4, 8, 16)
EPS = 1e-6

V7X_LANES = 128
V7X_SUBLANES = 8
V7X_MXU_COLS = 256
V7X_VMEM_LIMIT_BYTES = 60 * 1024 * 1024

SEQ_LO = 128
HALO = 16
TOKEN_TILE = 512
MIXER_SUB = 2
FF_CHUNK = V7X_MXU_COLS
W_IN_COL = 0
W_F_COL = W_IN_COL + 6 * GROUP
W_OUT_COL = W_F_COL + GROUP
W_GATE_COL = W_OUT_COL + D_MODEL
W_UP_COL = W_GATE_COL + D_FF
W_ROWS_COLS = W_UP_COL + D_FF
FOURIER_SUB = 4

F32 = jnp.float32
BF16 = jnp.bfloat16


def _dot(a, b):
    return jnp.dot(a, b, preferred_element_type=F32)


def _rms(x, g):
    ms = jnp.mean(x * x, axis=-1, keepdims=True)
    return x * lax.rsqrt(ms + EPS) * g


def _fourier_tables(seq):
    seq_hi = seq // SEQ_LO
    cm = np.outer(np.arange(HEAD_DIM), np.arange(HEAD_DIM)) % HEAD_DIM
    ang = 2.0 * np.pi * cm / HEAD_DIM
    cos_bd = np.kron(np.eye(N_HEADS), np.cos(ang)) / np.sqrt(HEAD_DIM)
    sin_bd = np.kron(np.eye(N_HEADS), np.sin(ang)) / np.sqrt(HEAD_DIM)
    fc = np.block([[cos_bd, -sin_bd], [sin_bd, cos_bd]])
    kt = np.outer(np.arange(seq_hi), np.arange(seq_hi)) % seq_hi
    ang1 = 2.0 * np.pi * kt / seq_hi
    eye = np.eye(V7X_SUBLANES)
    c1 = np.kron(np.cos(ang1) / np.sqrt(seq_hi), eye)
    s1 = np.kron(np.sin(ang1) / np.sqrt(seq_hi), eye)
    kd = np.concatenate([c1, -s1], axis=0)
    k = np.arange(SEQ_LO)[None, :, None] * seq_hi + np.arange(seq_hi)[:, None, None]
    ang2 = 2.0 * np.pi * ((k * np.arange(SEQ_LO)[None, None, :]) % seq) / seq
    e = np.concatenate([np.cos(ang2), np.sin(ang2)], axis=2) / np.sqrt(SEQ_LO)
    return (jnp.asarray(fc, F32).astype(BF16), jnp.asarray(kd, F32).astype(BF16),
            jnp.asarray(e, F32).astype(BF16))


def _fourier_kernel(x_ref, g_ref, wf_ref, fc_ref, kd_ref, e_ref, o_ref, y_s, il_s, f_s, *, seq_hi, n_tb):
    step = pl.program_id(0)
    tb = jnp.maximum(step - 1, 0) % n_tb
    rows = seq_hi * V7X_SUBLANES
    t_lo_per_step = V7X_SUBLANES * FOURIER_SUB

    @pl.when(step == 0)
    def _():
        f_s[...] = jnp.zeros_like(f_s)

    def stage_norm(q):
        x = x_ref[0, :, q * V7X_SUBLANES:(q + 1) * V7X_SUBLANES, :].reshape(rows, D_MODEL)
        inv = lax.rsqrt(jnp.mean(x * x, axis=-1, keepdims=True) + EPS)
        return (x * g_ref[0:1, :]).astype(BF16), inv

    def stage_proj(q, h_inv):
        h, inv = h_inv
        return (_dot(h, wf_ref[...]) * inv).astype(BF16)

    def stage_slow_dft(q, f):
        u = _dot(kd_ref[...], f)
        return jnp.concatenate([u[:rows], u[rows:]], axis=1).astype(BF16)

    def stage_channel_dft(q, uc):
        return _dot(uc, fc_ref[...])

    vals = []
    for q in range(FOURIER_SUB):
        h_inv = stage_norm(q)
        uc = stage_slow_dft(q, f_s[q])
        f_s[q] = stage_proj(q, h_inv)
        vals.append(stage_channel_dft(q, uc))

    row = pl.multiple_of(tb * t_lo_per_step, t_lo_per_step)
    for k in range(seq_hi):
        rs = slice(k * V7X_SUBLANES, (k + 1) * V7X_SUBLANES)
        ls = slice(k * GROUP, (k + 1) * GROUP)
        piece = jnp.concatenate([v[rs] for v in vals], axis=0).astype(BF16)
        y_s[pl.ds(row, t_lo_per_step), ls] = piece[:, :GROUP]
        y_s[pl.ds(SEQ_LO + row, t_lo_per_step), ls] = piece[:, GROUP:]

    @pl.when((step > 0) & (tb == n_tb - 1))
    def _():
        def out_block(kb, carry):
            for j in range(V7X_SUBLANES):
                k_lo = kb * V7X_SUBLANES + j
                res = _dot(e_ref[k_lo], y_s[:, pl.ds(pl.multiple_of(k_lo * GROUP, GROUP), GROUP)])
                for half in range(GROUP // V7X_LANES):
                    ls = slice(half * V7X_LANES, (half + 1) * V7X_LANES)
                    il_s[half, pl.ds(j, SEQ_LO, stride=V7X_SUBLANES), :] = res[:, ls]
            for half in range(GROUP // V7X_LANES):
                o_ref[half, 0, :, kb, :, :] = il_s[half].reshape(SEQ_LO, V7X_SUBLANES, V7X_LANES)
            return carry

        lax.fori_loop(0, seq_hi // V7X_SUBLANES, out_block, 0)


def _const_spec(shape):
    n = len(shape)
    return pl.BlockSpec(shape, lambda *_: (0,) * n, pipeline_mode=pl.Buffered(1))


def _layer_spec(layer, shape, block_index=None):
    index = (0,) * len(shape) if block_index is None else block_index
    return pl.BlockSpec((None,) + shape, lambda *_: (layer,) + index, pipeline_mode=pl.Buffered(1))


def _fourier_spec(x, layer, params, tables):
    b, seq, _ = x.shape
    seq_hi = seq // SEQ_LO
    t_lo_per_step = V7X_SUBLANES * FOURIER_SUB
    n_tb = SEQ_LO // t_lo_per_step
    rows = seq_hi * V7X_SUBLANES
    fc, kd, e = tables
    halves = GROUP // V7X_LANES
    k_lo_blocks = seq_hi // V7X_SUBLANES
    out_block = (halves, 1, SEQ_LO, k_lo_blocks, V7X_SUBLANES, V7X_LANES)
    x4 = x.reshape(b, seq_hi, SEQ_LO, D_MODEL)
    n_blocks = b * n_tb
    proj_block = lambda s: jnp.minimum(s, n_blocks - 1)
    dft_batch = lambda s: jnp.maximum(s - 1, 0) // n_tb
    out = pl.pallas_call(
        functools.partial(_fourier_kernel, seq_hi=seq_hi, n_tb=n_tb),
        grid=(n_blocks + 1,),
        in_specs=[
            pl.BlockSpec((1, seq_hi, t_lo_per_step, D_MODEL),
                         lambda s: (proj_block(s) // n_tb, 0, proj_block(s) % n_tb, 0)),
            _layer_spec(layer, (V7X_SUBLANES, D_MODEL)),
            _layer_spec(layer, (D_MODEL, GROUP), (0, W_F_COL // GROUP)),
            _const_spec((2 * GROUP, 2 * GROUP)),
            _const_spec((2 * rows, rows)),
            _const_spec((seq_hi, SEQ_LO, 2 * SEQ_LO)),
        ],
        out_specs=pl.BlockSpec(out_block, lambda s: (0, dft_batch(s), 0, 0, 0, 0)),
        out_shape=jax.ShapeDtypeStruct((halves, b) + out_block[2:], F32),
        scratch_shapes=[
            pltpu.VMEM((2 * SEQ_LO, seq_hi * GROUP), BF16),
            pltpu.VMEM((halves, SEQ_LO * V7X_SUBLANES, V7X_LANES), F32),
            pltpu.VMEM((FOURIER_SUB, rows, GROUP), BF16),
        ],
        compiler_params=pltpu.CompilerParams(
            dimension_semantics=("arbitrary",),
            vmem_limit_bytes=V7X_VMEM_LIMIT_BYTES),
        name="fourier_spec",
    )(x4, params["gains"], params["w_rows"], fc, kd, e)
    return out.reshape(halves, b * seq, V7X_LANES)


def _mixer_ffn_kernel(x_ref, xp_ref, xn_ref, spec_ref, gains_ref, vec_ref, mats_ref, wcat_ref, w_rows_ref,
                      wd_ref,
                      o_ref, cz_s, p_s, ps_s, act_s, x1_all_s, h2_all_s, *, tile, seq, n_blocks):
    g_pre_ref, g_post_ref, g_pre_ffn_ref, g_post_ffn_ref, g_grp_ref = (
        gains_ref.at[k:k + 1] for k in range(5))
    sbias_ref = vec_ref.at[0:CHUNK]
    conv_w_ref = vec_ref.at[CHUNK:CHUNK + 3]
    pool_scale_ref = vec_ref.at[CHUNK + 3:CHUNK + 4]
    pool_w_ref, four_w_ref, mavg_ref = (mats_ref.at[k] for k in range(3))
    w_in_ref = w_rows_ref.at[:, W_IN_COL:W_IN_COL + 6 * GROUP]
    w_out_ref = w_rows_ref.at[:, W_OUT_COL:W_OUT_COL + D_MODEL]
    wg_ref = w_rows_ref.at[:, W_GATE_COL:W_GATE_COL + D_FF]
    wu_ref = w_rows_ref.at[:, W_UP_COL:W_UP_COL + D_FF]
    step = pl.program_id(0)

    @pl.when(step == 0)
    def _():
        x1_all_s[...] = jnp.zeros_like(x1_all_s)
        h2_all_s[...] = jnp.zeros_like(h2_all_s)

    blk = jnp.minimum(step, n_blocks - 1)
    tiles_per_seq = seq // tile
    for k in range(MIXER_SUB):
        rows_k = pl.ds(k * tile, tile)
        prev_ref = xp_ref if k == 0 else x_ref.at[pl.ds(k * tile - HALO, HALO)]
        next_ref = xn_ref if k == MIXER_SUB - 1 else x_ref.at[pl.ds((k + 1) * tile, HALO)]
        _mixer_ffn_tile(
            x_ref.at[rows_k], prev_ref, next_ref, spec_ref.at[:, rows_k], o_ref.at[rows_k],
            x1_all_s.at[k], h2_all_s.at[k], (blk * MIXER_SUB + k) % tiles_per_seq, tiles_per_seq,
            g_pre_ref, g_post_ref, g_pre_ffn_ref, g_post_ffn_ref, g_grp_ref, sbias_ref, conv_w_ref,
            pool_scale_ref, pool_w_ref, four_w_ref, mavg_ref, wcat_ref, w_in_ref, w_out_ref, wg_ref, wu_ref,
            wd_ref, cz_s, p_s, ps_s, act_s, tile=tile, seq=seq)


def _mixer_ffn_tile(x_ref, xp_ref, xn_ref, spec_ref, o_ref, x1_s, h2_s, ti, tiles_per_seq,
                    g_pre_ref, g_post_ref, g_pre_ffn_ref, g_post_ffn_ref, g_grp_ref, sbias_ref, conv_w_ref,
                    pool_scale_ref, pool_w_ref, four_w_ref, mavg_ref, wcat_ref, w_in_ref, w_out_ref, wg_ref,
                    wu_ref, wd_ref, cz_s, p_s, ps_s, act_s, *, tile, seq):
    def ffn_chunks(lo, hi):
        for j in range(lo, hi):
            ls = slice(j * FF_CHUNK, (j + 1) * FF_CHUNK)
            gate = _dot(h2_s[...], wg_ref[:, ls])
            up = _dot(h2_s[...], wu_ref[:, ls])
            act_s[:, ls] = (gate * jax.nn.sigmoid(gate) * up).astype(BF16)

    def ffn_down(rows):
        f = _dot(act_s[rows, :], wd_ref[...])
        o_ref[rows, :] = o_ref[rows, :] + _rms(f, g_post_ffn_ref[...])

    o_ref[...] = x1_s[...]
    ffn_chunks(0, 1)

    x = x_ref[...]
    xh = jnp.concatenate([xp_ref[...], xn_ref[...]], axis=0)

    hm = _rms(x, g_pre_ref[...]).astype(BF16)
    hh = _rms(xh, g_pre_ref[...]).astype(BF16)
    ffn_chunks(1, 2)
    pe = _dot(jnp.concatenate([hm, hh], axis=0), w_in_ref[:, :3 * GROUP])
    pm = jnp.concatenate([pe[:tile], _dot(hm, w_in_ref[:, 3 * GROUP:])], axis=1)
    ph = pe[tile:]
    ffn_chunks(2, 3)
    hrow = lax.broadcasted_iota(jnp.int32, (2 * HALO, 1), 0)
    has_prev = (ti != 0).astype(jnp.int32)
    has_next = (ti != tiles_per_seq - 1).astype(jnp.int32)
    ph = jnp.where(jnp.where(hrow < HALO, has_prev, has_next) != 0, ph, 0.0)

    cz_s[pl.ds(0, HALO), :] = ph[:HALO, :GROUP] * ph[:HALO, GROUP:2 * GROUP]
    cz_s[pl.ds(HALO, tile), :] = pm[:, :GROUP] * pm[:, GROUP:2 * GROUP]
    cz_s[pl.ds(HALO + tile, HALO), :] = ph[HALO:, :GROUP] * ph[HALO:, GROUP:2 * GROUP]
    p_s[pl.ds(0, HALO), :] = ph[:HALO, 2 * GROUP:]
    p_s[pl.ds(HALO, tile), :] = pm[:, 2 * GROUP:3 * GROUP]
    p_s[pl.ds(HALO + tile, HALO), :] = ph[HALO:, 2 * GROUP:]
    p_s[pl.ds(2 * HALO + tile, V7X_SUBLANES), :] = jnp.zeros((V7X_SUBLANES, GROUP), F32)
    b_gate = pm[:, 3 * GROUP:4 * GROUP]
    u = pm[:, 4 * GROUP:5 * GROUP]
    v = pm[:, 5 * GROUP:]

    y_conv = b_gate * (conv_w_ref[0:1, :] * cz_s[pl.ds(HALO - 1, tile), :]
                       + conv_w_ref[1:2, :] * cz_s[pl.ds(HALO, tile), :]
                       + conv_w_ref[2:3, :] * cz_s[pl.ds(HALO + 1, tile), :])
    ffn_chunks(3, 4)

    pos = ti * tile + lax.broadcasted_iota(jnp.int32, (tile, 1), 0)
    lane = lax.broadcasted_iota(jnp.int32, (1, V7X_LANES), 1)
    first_half = lane < HEAD_DIM
    def pool_col(col, inner, outer):
        w_small, w_big = POOL_WINDOWS[2 * col], POOL_WINDOWS[2 * col + 1]
        half = jnp.where(first_half, w_small // 2, w_big // 2)
        cnt = (jnp.minimum(pos + half, seq) - jnp.maximum(pos - half, 0)).astype(F32)
        mean = jnp.where(first_half, inner, outer) / cnt
        return mean - p_s[pl.ds(HALO, tile), col * V7X_LANES:(col + 1) * V7X_LANES]

    ls0 = slice(0, V7X_LANES)
    inner0 = p_s[pl.ds(HALO - 1, tile), ls0] + p_s[pl.ds(HALO, tile), ls0]
    outer0 = inner0 + p_s[pl.ds(HALO - 2, tile), ls0] + p_s[pl.ds(HALO + 1, tile), ls0]
    d_col0 = pool_col(0, inner0, outer0)
    ffn_chunks(4, 5)
    ls1 = slice(V7X_LANES, 2 * V7X_LANES)
    base = HALO - POOL_WINDOWS[3] // 2
    n2, n4, n8 = tile + 24, tile + 16, tile + 8
    ps_s[0, pl.ds(base, n2), :] = p_s[pl.ds(base, n2), ls1] + p_s[pl.ds(base + 1, n2), ls1]
    ps_s[1, pl.ds(base, n4), :] = ps_s[0, pl.ds(base, n4), :] + ps_s[0, pl.ds(base + 2, n4), :]
    ps_s[0, pl.ds(base, n8), :] = ps_s[1, pl.ds(base, n8), :] + ps_s[1, pl.ds(base + 4, n8), :]
    inner1 = ps_s[0, pl.ds(HALO - 4, tile), :]
    outer1 = ps_s[0, pl.ds(HALO - 8, tile), :] + ps_s[0, pl.ds(HALO, tile), :]
    d = jnp.concatenate([d_col0, pool_col(1, inner1, outer1)], axis=1).astype(BF16)
    y_pool = _dot(d, pool_w_ref[...]) * pool_scale_ref[...]

    spec = jnp.concatenate([spec_ref[k] for k in range(GROUP // V7X_LANES)], axis=1).astype(BF16)
    y_four = _dot(spec, four_w_ref[...])
    ffn_chunks(5, 6)

    mu = _dot(v.astype(BF16), mavg_ref[...])
    vc = v - mu
    var = _dot((vc * vc).astype(BF16), mavg_ref[...])
    vh = vc * lax.rsqrt(var + EPS)
    ffn_chunks(6, 7)
    lane_g = lax.broadcasted_iota(jnp.int32, (1, GROUP), 1)
    s_chunks = []
    for n in range(tile // CHUNK):
        vn = vh[n * CHUNK:(n + 1) * CHUNK]
        stack = jnp.concatenate(
            [jnp.where(lane_g // HEAD_DIM == hd, vn, 0.0).astype(BF16) for hd in range(N_HEADS)], axis=0)
        s_chunks.append(_dot(wcat_ref[...], stack) + sbias_ref[...])
    y_gmlp = u * jnp.concatenate(s_chunks, axis=0)
    ffn_chunks(7, 8)

    y = jnp.concatenate(
        [_rms(yk, g_grp_ref[:, k * GROUP:(k + 1) * GROUP]).astype(BF16)
         for k, yk in enumerate((y_conv, y_pool, y_four, y_gmlp))], axis=1)
    ffn_chunks(8, 9)

    def out_half(rows):
        m = _dot(y[rows, :], w_out_ref[...])
        x1 = x_ref[rows, :] + _rms(m, g_post_ref[...])
        return x1, _rms(x1, g_pre_ffn_ref[...]).astype(BF16)

    top, bot = slice(0, tile // 2), slice(tile // 2, tile)
    x1_top, h2_top = out_half(top)
    ffn_chunks(9, 10)
    x1_bot, h2_bot = out_half(bot)
    ffn_chunks(10, D_FF // FF_CHUNK)
    ffn_down(top)
    x1_s[top, :] = x1_top
    h2_s[top, :] = h2_top
    x1_s[bot, :] = x1_bot
    h2_s[bot, :] = h2_bot
    ffn_down(bot)


def _mixer_ffn(x2, spec, seq, layer, p):
    n_tok = x2.shape[0]
    tile = min(TOKEN_TILE, seq)
    block = MIXER_SUB * tile
    n_blocks = n_tok // block
    n_halo_blocks = n_tok // HALO
    per_block = block // HALO
    cur = lambda i: jnp.minimum(i, n_blocks - 1)
    return pl.pallas_call(
        functools.partial(_mixer_ffn_kernel, tile=tile, seq=seq, n_blocks=n_blocks),
        grid=(n_blocks + 1,),
        in_specs=[
            pl.BlockSpec((block, D_MODEL), lambda i: (cur(i), 0)),
            pl.BlockSpec((HALO, D_MODEL), lambda i: (jnp.maximum(cur(i) * per_block - 1, 0), 0)),
            pl.BlockSpec((HALO, D_MODEL),
                         lambda i: (jnp.minimum((cur(i) + 1) * per_block, n_halo_blocks - 1), 0)),
            pl.BlockSpec((GROUP // V7X_LANES, block, V7X_LANES), lambda i: (0, cur(i), 0)),
            _layer_spec(layer, p["gains"].shape[1:]),
            _layer_spec(layer, p["vec"].shape[1:]),
            _layer_spec(layer, p["mats"].shape[1:]),
            _layer_spec(layer, p["wcat"].shape[1:]),
            _layer_spec(layer, p["w_rows"].shape[1:]),
            _layer_spec(layer, p["wd"].shape[1:]),
        ],
        out_specs=pl.BlockSpec((block, D_MODEL), lambda i: (jnp.maximum(i - 1, 0), 0)),
        out_shape=jax.ShapeDtypeStruct(x2.shape, x2.dtype),
        scratch_shapes=[
            pltpu.VMEM((tile + 2 * HALO, GROUP), F32),
            pltpu.VMEM((tile + 2 * HALO + V7X_SUBLANES, GROUP), F32),
            pltpu.VMEM((2, tile + 2 * HALO, V7X_LANES), F32),
            pltpu.VMEM((tile, D_FF), BF16),
            pltpu.VMEM((MIXER_SUB, tile, D_MODEL), F32),
            pltpu.VMEM((MIXER_SUB, tile, D_MODEL), BF16),
        ],
        compiler_params=pltpu.CompilerParams(
            dimension_semantics=("arbitrary",),
            vmem_limit_bytes=V7X_VMEM_LIMIT_BYTES),
        name="mixer_ffn",
    )(x2, x2, x2, spec, p["gains"], p["vec"], p["mats"], p["wcat"], p["w_rows"], p["wd"])


def _block_diag(blocks):
    depth, n, k, _ = blocks.shape
    on_diagonal = jnp.eye(n, dtype=bool)[None, :, None, :, None]
    return jnp.where(on_diagonal, blocks[:, :, :, None, :], 0.0).reshape(depth, n * k, n * k)


def _pack_params(pre_mix_gain, post_mix_gain, pre_ffn_gain, post_ffn_gain, w_in, conv_w, pool_w, pool_scale,
                 fourier_w, spatial_w, spatial_b, group_norm_gain, w_out, w_gate, w_up, w_down):
    depth = w_in.shape[0]
    sl = lambda k: w_in[:, :, k * GROUP:(k + 1) * GROUP]
    w_rows = jnp.concatenate([sl(1), sl(2), sl(3), sl(0), sl(5), sl(6), sl(4), w_out, w_gate, w_up],
                             axis=2).astype(BF16)
    assert w_rows.shape[2] == W_ROWS_COLS
    mavg = np.kron(np.eye(N_HEADS), np.full((HEAD_DIM, HEAD_DIM), 1.0 / HEAD_DIM))
    mavg = jnp.broadcast_to(jnp.asarray(mavg, F32), (depth, GROUP, GROUP))
    sbias = jnp.repeat(jnp.transpose(spatial_b, (0, 2, 1)), HEAD_DIM, axis=2)
    row = lambda g: g[:, None, :]
    return dict(
        gains=jnp.concatenate([row(pre_mix_gain), row(post_mix_gain), row(pre_ffn_gain), row(post_ffn_gain),
                               row(group_norm_gain), jnp.zeros((depth, V7X_SUBLANES - 5, D_MODEL), F32)],
                              axis=1),
        vec=jnp.concatenate([sbias, conv_w, row(pool_scale),
                             jnp.zeros((depth, V7X_SUBLANES - 4, GROUP), F32)], axis=1),
        mats=jnp.stack([_block_diag(pool_w), _block_diag(fourier_w), mavg], axis=1).astype(BF16),
        wcat=jnp.transpose(spatial_w, (0, 2, 1, 3)).reshape(depth, CHUNK, N_HEADS * CHUNK).astype(BF16),
        w_rows=w_rows,
        wd=w_down.astype(BF16),
    )


def kernel(x, pre_mix_gain, post_mix_gain, pre_ffn_gain, post_ffn_gain, w_in, conv_w, pool_w, pool_scale,
           fourier_w, spatial_w, spatial_b, group_norm_gain, w_out, w_gate, w_up, w_down):
    b, seq, d = x.shape
    assert d == D_MODEL and seq % SEQ_LO == 0 and seq % (MIXER_SUB * min(TOKEN_TILE, seq)) == 0
    tables = _fourier_tables(seq)
    p = _pack_params(pre_mix_gain, post_mix_gain, pre_ffn_gain, post_ffn_gain, w_in, conv_w, pool_w,
                     pool_scale, fourier_w, spatial_w, spatial_b, group_norm_gain, w_out, w_gate, w_up,
                     w_down)
    for layer in range(w_in.shape[0]):
        spec = _fourier_spec(x, layer, p, tables)
        x = _mixer_ffn(x.reshape(b * seq, d), spec, seq, layer, p).reshape(b, seq, d)
    return x
```

```python
import functools

import jax
import jax.numpy as jnp
import numpy as np
from jax import lax
from jax.experimental import pallas as pl
from jax.experimental.pallas import tpu as pltpu

D_MODEL = 1024
GROUP = 256
HEAD_DIM = 64
N_HEADS = GROUP // HEAD_DIM
CHUNK = 128
D_FF = 2816
POOL_WINDOWS = (2, 4, 8, 16)
EPS = 1e-6

V7X_LANES = 128
V7X_SUBLANES = 8
V7X_MXU_COLS = 256
V7X_VMEM_LIMIT_BYTES = 56 * 1024 * 1024

SEQ_LO = 128
HALO = 16
TOKEN_TILE = 512
MIXER_SUB = 1
FF_CHUNK = V7X_MXU_COLS
W_IN_COL = 0
W_F_COL = W_IN_COL + 6 * GROUP
W_OUT_COL = W_F_COL + GROUP
W_GATE_COL = W_OUT_COL + D_MODEL
W_UP_COL = W_GATE_COL + D_FF
W_ROWS_COLS = W_UP_COL + D_FF
FOURIER_SUB = 4

F32 = jnp.float32
BF16 = jnp.bfloat16


def _dot(a, b):
    return jnp.dot(a, b, preferred_element_type=F32)


def _rms(x, g):
    ms = jnp.mean(x * x, axis=-1, keepdims=True)
    return x * lax.rsqrt(ms + EPS) * g


def _fourier_tables(seq):
    seq_hi = seq // SEQ_LO
    cm = np.outer(np.arange(HEAD_DIM), np.arange(HEAD_DIM)) % HEAD_DIM
    ang = 2.0 * np.pi * cm / HEAD_DIM
    cos_bd = np.kron(np.eye(N_HEADS), np.cos(ang)) / np.sqrt(HEAD_DIM)
    sin_bd = np.kron(np.eye(N_HEADS), np.sin(ang)) / np.sqrt(HEAD_DIM)
    fc = np.block([[cos_bd, -sin_bd], [sin_bd, cos_bd]])
    kt = np.outer(np.arange(seq_hi), np.arange(seq_hi)) % seq_hi
    ang1 = 2.0 * np.pi * kt / seq_hi
    eye = np.eye(V7X_SUBLANES)
    c1 = np.kron(np.cos(ang1) / np.sqrt(seq_hi), eye)
    s1 = np.kron(np.sin(ang1) / np.sqrt(seq_hi), eye)
    kd = np.concatenate([c1, -s1], axis=0)
    k = np.arange(SEQ_LO)[None, :, None] * seq_hi + np.arange(seq_hi)[:, None, None]
    ang2 = 2.0 * np.pi * ((k * np.arange(SEQ_LO)[None, None, :]) % seq) / seq
    e = np.concatenate([np.cos(ang2), np.sin(ang2)], axis=2) / np.sqrt(SEQ_LO)
    return (jnp.asarray(fc, F32).astype(BF16), jnp.asarray(kd, F32).astype(BF16),
            jnp.asarray(e, F32).astype(BF16))


def _fourier_kernel(xa_ref, xb_ref, g_ref, wf_ref, fc_ref, kd_ref, e_ref, o_ref, y_s, il_s, f_s,
                    *, seq_hi, n_tb):
    step = pl.program_id(0)
    tb = jnp.maximum(step - 1, 0) % n_tb
    rows = seq_hi * V7X_SUBLANES
    t_lo_per_step = V7X_SUBLANES * FOURIER_SUB

    @pl.when(step == 0)
    def _():
        f_s[...] = jnp.zeros_like(f_s)

    def stage_norm(q):
        js = slice(q * V7X_SUBLANES, (q + 1) * V7X_SUBLANES)
        x = jnp.concatenate([xa_ref[0, :, js, :], xb_ref[0, :, js, :]], axis=0).reshape(rows, D_MODEL)
        inv = lax.rsqrt(jnp.mean(x * x, axis=-1, keepdims=True) + EPS)
        return (x * g_ref[0:1, :]).astype(BF16), inv

    def stage_proj(q, h_inv):
        h, inv = h_inv
        return (_dot(h, wf_ref[...]) * inv).astype(BF16)

    def stage_slow_dft(q, f):
        u = _dot(kd_ref[...], f)
        return jnp.concatenate([u[:rows], u[rows:]], axis=1).astype(BF16)

    def stage_channel_dft(q, uc):
        return _dot(uc, fc_ref[...])

    vals = []
    for q in range(FOURIER_SUB):
        h_inv = stage_norm(q)
        uc = stage_slow_dft(q, f_s[q])
        f_s[q] = stage_proj(q, h_inv)
        vals.append(stage_channel_dft(q, uc))

    row = pl.multiple_of(tb * t_lo_per_step, t_lo_per_step)
    for k in range(seq_hi):
        rs = slice(k * V7X_SUBLANES, (k + 1) * V7X_SUBLANES)
        ls = slice(k * GROUP, (k + 1) * GROUP)
        piece = jnp.concatenate([v[rs] for v in vals], axis=0).astype(BF16)
        y_s[pl.ds(row, t_lo_per_step), ls] = piece[:, :GROUP]
        y_s[pl.ds(SEQ_LO + row, t_lo_per_step), ls] = piece[:, GROUP:]

    @pl.when((step > 0) & (tb == n_tb - 1))
    def _():
        def out_block(kb, carry):
            for j in range(V7X_SUBLANES):
                k_lo = kb * V7X_SUBLANES + j
                res = _dot(e_ref[k_lo], y_s[:, pl.ds(pl.multiple_of(k_lo * GROUP, GROUP), GROUP)])
                for half in range(GROUP // V7X_LANES):
                    ls = slice(half * V7X_LANES, (half + 1) * V7X_LANES)
                    il_s[half, pl.ds(j, SEQ_LO, stride=V7X_SUBLANES), :] = res[:, ls]
            for half in range(GROUP // V7X_LANES):
                o_ref[half, 0, :, kb, :, :] = il_s[half].reshape(SEQ_LO, V7X_SUBLANES, V7X_LANES)
            return carry

        lax.fori_loop(0, seq_hi // V7X_SUBLANES, out_block, 0)


def _const_spec(shape):
    n = len(shape)
    return pl.BlockSpec(shape, lambda *_: (0,) * n, pipeline_mode=pl.Buffered(1))


def _layer_spec(layer, shape, block_index=None):
    index = (0,) * len(shape) if block_index is None else block_index
    return pl.BlockSpec((None,) + shape, lambda *_: (layer,) + index, pipeline_mode=pl.Buffered(1))


def _fourier_spec(x, layer, params, tables):
    b, seq, _ = x.shape
    seq_hi = seq // SEQ_LO
    t_lo_per_step = V7X_SUBLANES * FOURIER_SUB
    n_tb = SEQ_LO // t_lo_per_step
    rows = seq_hi * V7X_SUBLANES
    fc, kd, e = tables
    halves = GROUP // V7X_LANES
    k_lo_blocks = seq_hi // V7X_SUBLANES
    out_block = (halves, 1, SEQ_LO, k_lo_blocks, V7X_SUBLANES, V7X_LANES)
    x4 = x.reshape(b, seq_hi, SEQ_LO, D_MODEL)
    n_blocks = b * n_tb
    proj_block = lambda s: jnp.minimum(s, n_blocks - 1)
    dft_batch = lambda s: jnp.maximum(s - 1, 0) // n_tb
    out = pl.pallas_call(
        functools.partial(_fourier_kernel, seq_hi=seq_hi, n_tb=n_tb),
        grid=(n_blocks + 1,),
        in_specs=[
            pl.BlockSpec((1, seq_hi // 2, t_lo_per_step, D_MODEL),
                         lambda s: (proj_block(s) // n_tb, 0, proj_block(s) % n_tb, 0)),
            pl.BlockSpec((1, seq_hi // 2, t_lo_per_step, D_MODEL),
                         lambda s: (proj_block(s) // n_tb, 1, proj_block(s) % n_tb, 0)),
            _layer_spec(layer, (V7X_SUBLANES, D_MODEL)),
            _layer_spec(layer, (D_MODEL, GROUP), (0, W_F_COL // GROUP)),
            _const_spec((2 * GROUP, 2 * GROUP)),
            _const_spec((2 * rows, rows)),
            _const_spec((seq_hi, SEQ_LO, 2 * SEQ_LO)),
        ],
        out_specs=pl.BlockSpec(out_block, lambda s: (0, dft_batch(s), 0, 0, 0, 0)),
        out_shape=jax.ShapeDtypeStruct((halves, b) + out_block[2:], F32),
        scratch_shapes=[
            pltpu.VMEM((2 * SEQ_LO, seq_hi * GROUP), BF16),
            pltpu.VMEM((halves, SEQ_LO * V7X_SUBLANES, V7X_LANES), F32),
            pltpu.VMEM((FOURIER_SUB, rows, GROUP), BF16),
        ],
        compiler_params=pltpu.CompilerParams(
            dimension_semantics=("arbitrary",),
            vmem_limit_bytes=V7X_VMEM_LIMIT_BYTES),
        name="fourier_spec",
    )(x4, x4, params["gains"], params["w_rows"], fc, kd, e)
    return out.reshape(halves, b * seq, V7X_LANES)


def _mixer_ffn_kernel(x_ref, xp_ref, xn_ref, spec_ref, gains_ref, vec_ref, mats_ref, wcat_ref, w_rows_ref,
                      wd_ref,
                      o_ref, cz_s, p_s, ps_s, act_s, x1_all_s, h2_all_s, *, tile, seq, n_blocks):
    g_pre_ref, g_post_ref, g_pre_ffn_ref, g_post_ffn_ref, g_grp_ref = (
        gains_ref.at[k:k + 1] for k in range(5))
    sbias_ref = vec_ref.at[0:CHUNK]
    conv_w_ref = vec_ref.at[CHUNK:CHUNK + 3]
    pool_scale_ref = vec_ref.at[CHUNK + 3:CHUNK + 4]
    pool_w_ref, four_w_ref, mavg_ref = (mats_ref.at[k] for k in range(3))
    w_in_ref = w_rows_ref.at[:, W_IN_COL:W_IN_COL + 6 * GROUP]
    w_out_ref = w_rows_ref.at[:, W_OUT_COL:W_OUT_COL + D_MODEL]
    wg_ref = w_rows_ref.at[:, W_GATE_COL:W_GATE_COL + D_FF]
    wu_ref = w_rows_ref.at[:, W_UP_COL:W_UP_COL + D_FF]
    step = pl.program_id(0)

    @pl.when(step == 0)
    def _():
        x1_all_s[...] = jnp.zeros_like(x1_all_s)
        h2_all_s[...] = jnp.zeros_like(h2_all_s)

    blk = jnp.minimum(step, n_blocks - 1)
    tiles_per_seq = seq // tile
    for k in range(MIXER_SUB):
        rows_k = pl.ds(k * tile, tile)
        prev_ref = xp_ref if k == 0 else x_ref.at[pl.ds(k * tile - HALO, HALO)]
        next_ref = xn_ref if k == MIXER_SUB - 1 else x_ref.at[pl.ds((k + 1) * tile, HALO)]
        _mixer_ffn_tile(
            x_ref.at[rows_k], prev_ref, next_ref, spec_ref.at[:, rows_k], o_ref.at[rows_k],
            x1_all_s.at[k], h2_all_s.at[k], (blk * MIXER_SUB + k) % tiles_per_seq, tiles_per_seq,
            g_pre_ref, g_post_ref, g_pre_ffn_ref, g_post_ffn_ref, g_grp_ref, sbias_ref, conv_w_ref,
            pool_scale_ref, pool_w_ref, four_w_ref, mavg_ref, wcat_ref, w_in_ref, w_out_ref, wg_ref, wu_ref,
            wd_ref, cz_s, p_s, ps_s, act_s, tile=tile, seq=seq)


def _mixer_ffn_tile(x_ref, xp_ref, xn_ref, spec_ref, o_ref, x1_s, h2_s, ti, tiles_per_seq,
                    g_pre_ref, g_post_ref, g_pre_ffn_ref, g_post_ffn_ref, g_grp_ref, sbias_ref, conv_w_ref,
                    pool_scale_ref, pool_w_ref, four_w_ref, mavg_ref, wcat_ref, w_in_ref, w_out_ref, wg_ref,
                    wu_ref, wd_ref, cz_s, p_s, ps_s, act_s, *, tile, seq):
    def ffn_chunks(lo, hi):
        for j in range(lo, hi):
            ls = slice(j * FF_CHUNK, (j + 1) * FF_CHUNK)
            gate = _dot(h2_s[...], wg_ref[:, ls])
            up = _dot(h2_s[...], wu_ref[:, ls])
            act_s[:, ls] = (gate * jax.nn.sigmoid(gate) * up).astype(BF16)

    def ffn_down(rows):
        f = _dot(act_s[rows, :], wd_ref[...])
        o_ref[rows, :] = o_ref[rows, :] + _rms(f, g_post_ffn_ref[...])

    o_ref[...] = x1_s[...]
    ffn_chunks(0, 1)

    x = x_ref[...]
    xh = jnp.concatenate([xp_ref[...], xn_ref[...]], axis=0)

    hm = _rms(x, g_pre_ref[...]).astype(BF16)
    hh = _rms(xh, g_pre_ref[...]).astype(BF16)
    ffn_chunks(1, 2)
    pe = _dot(jnp.concatenate([hm, hh], axis=0), w_in_ref[:, :3 * GROUP])
    pm = jnp.concatenate([pe[:tile], _dot(hm, w_in_ref[:, 3 * GROUP:])], axis=1)
    ph = pe[tile:]
    ffn_chunks(2, 3)
    hrow = lax.broadcasted_iota(jnp.int32, (2 * HALO, 1), 0)
    has_prev = (ti != 0).astype(jnp.int32)
    has_next = (ti != tiles_per_seq - 1).astype(jnp.int32)
    ph = jnp.where(jnp.where(hrow < HALO, has_prev, has_next) != 0, ph, 0.0)

    cz_s[pl.ds(0, HALO), :] = ph[:HALO, :GROUP] * ph[:HALO, GROUP:2 * GROUP]
    cz_s[pl.ds(HALO, tile), :] = pm[:, :GROUP] * pm[:, GROUP:2 * GROUP]
    cz_s[pl.ds(HALO + tile, HALO), :] = ph[HALO:, :GROUP] * ph[HALO:, GROUP:2 * GROUP]
    p_s[pl.ds(0, HALO), :] = ph[:HALO, 2 * GROUP:]
    p_s[pl.ds(HALO, tile), :] = pm[:, 2 * GROUP:3 * GROUP]
    p_s[pl.ds(HALO + tile, HALO), :] = ph[HALO:, 2 * GROUP:]
    p_s[pl.ds(2 * HALO + tile, V7X_SUBLANES), :] = jnp.zeros((V7X_SUBLANES, GROUP), F32)
    b_gate = pm[:, 3 * GROUP:4 * GROUP]
    u = pm[:, 4 * GROUP:5 * GROUP]
    v = pm[:, 5 * GROUP:]

    y_conv = b_gate * (conv_w_ref[0:1, :] * cz_s[pl.ds(HALO - 1, tile), :]
                       + conv_w_ref[1:2, :] * cz_s[pl.ds(HALO, tile), :]
                       + conv_w_ref[2:3, :] * cz_s[pl.ds(HALO + 1, tile), :])
    ffn_chunks(3, 4)

    pos = ti * tile + lax.broadcasted_iota(jnp.int32, (tile, 1), 0)
    lane = lax.broadcasted_iota(jnp.int32, (1, V7X_LANES), 1)
    first_half = lane < HEAD_DIM
    def pool_col(col, inner, outer):
        w_small, w_big = POOL_WINDOWS[2 * col], POOL_WINDOWS[2 * col + 1]
        half = jnp.where(first_half, w_small // 2, w_big // 2)
        cnt = (jnp.minimum(pos + half, seq) - jnp.maximum(pos - half, 0)).astype(F32)
        mean = jnp.where(first_half, inner, outer) / cnt
        return mean - p_s[pl.ds(HALO, tile), col * V7X_LANES:(col + 1) * V7X_LANES]

    ls0 = slice(0, V7X_LANES)
    inner0 = p_s[pl.ds(HALO - 1, tile), ls0] + p_s[pl.ds(HALO, tile), ls0]
    outer0 = inner0 + p_s[pl.ds(HALO - 2, tile), ls0] + p_s[pl.ds(HALO + 1, tile), ls0]
    d_col0 = pool_col(0, inner0, outer0)
    ffn_chunks(4, 5)
    ls1 = slice(V7X_LANES, 2 * V7X_LANES)
    base = HALO - POOL_WINDOWS[3] // 2
    n2, n4, n8 = tile + 24, tile + 16, tile + 8
    ps_s[0, pl.ds(base, n2), :] = p_s[pl.ds(base, n2), ls1] + p_s[pl.ds(base + 1, n2), ls1]
    ps_s[1, pl.ds(base, n4), :] = ps_s[0, pl.ds(base, n4), :] + ps_s[0, pl.ds(base + 2, n4), :]
    ps_s[0, pl.ds(base, n8), :] = ps_s[1, pl.ds(base, n8), :] + ps_s[1, pl.ds(base + 4, n8), :]
    inner1 = ps_s[0, pl.ds(HALO - 4, tile), :]
    outer1 = ps_s[0, pl.ds(HALO - 8, tile), :] + ps_s[0, pl.ds(HALO, tile), :]
    d = jnp.concatenate([d_col0, pool_col(1, inner1, outer1)], axis=1).astype(BF16)
    y_pool = _dot(d, pool_w_ref[...]) * pool_scale_ref[...]

    spec = jnp.concatenate([spec_ref[k] for k in range(GROUP // V7X_LANES)], axis=1).astype(BF16)
    y_four = _dot(spec, four_w_ref[...])
    ffn_chunks(5, 6)

    mu = _dot(v.astype(BF16), mavg_ref[...])
    vc = v - mu
    var = _dot((vc * vc).astype(BF16), mavg_ref[...])
    vh = vc * lax.rsqrt(var + EPS)
    ffn_chunks(6, 7)
    lane_g = lax.broadcasted_iota(jnp.int32, (1, GROUP), 1)
    s_chunks = []
    for n in range(tile // CHUNK):
        vn = vh[n * CHUNK:(n + 1) * CHUNK]
        stack = jnp.concatenate(
            [jnp.where(lane_g // HEAD_DIM == hd, vn, 0.0).astype(BF16) for hd in range(N_HEADS)], axis=0)
        s_chunks.append(_dot(wcat_ref[...], stack) + sbias_ref[...])
    y_gmlp = u * jnp.concatenate(s_chunks, axis=0)
    ffn_chunks(7, 8)

    y = jnp.concatenate(
        [_rms(yk, g_grp_ref[:, k * GROUP:(k + 1) * GROUP]).astype(BF16)
         for k, yk in enumerate((y_conv, y_pool, y_four, y_gmlp))], axis=1)
    ffn_chunks(8, 9)

    def out_half(rows):
        m = _dot(y[rows, :], w_out_ref[...])
        x1 = x_ref[rows, :] + _rms(m, g_post_ref[...])
        return x1, _rms(x1, g_pre_ffn_ref[...]).astype(BF16)

    top, bot = slice(0, tile // 2), slice(tile // 2, tile)
    x1_top, h2_top = out_half(top)
    ffn_chunks(9, 10)
    x1_bot, h2_bot = out_half(bot)
    ffn_chunks(10, D_FF // FF_CHUNK)
    ffn_down(top)
    x1_s[top, :] = x1_top
    h2_s[top, :] = h2_top
    x1_s[bot, :] = x1_bot
    h2_s[bot, :] = h2_bot
    ffn_down(bot)


def _mixer_ffn(x2, spec, seq, layer, p):
    n_tok = x2.shape[0]
    tile = min(TOKEN_TILE, seq)
    block = MIXER_SUB * tile
    n_blocks = n_tok // block
    n_halo_blocks = n_tok // HALO
    per_block = block // HALO
    cur = lambda i: jnp.minimum(i, n_blocks - 1)
    return pl.pallas_call(
        functools.partial(_mixer_ffn_kernel, tile=tile, seq=seq, n_blocks=n_blocks),
        grid=(n_blocks + 1,),
        in_specs=[
            pl.BlockSpec((block, D_MODEL), lambda i: (cur(i), 0)),
            pl.BlockSpec((HALO, D_MODEL), lambda i: (jnp.maximum(cur(i) * per_block - 1, 0), 0)),
            pl.BlockSpec((HALO, D_MODEL),
                         lambda i: (jnp.minimum((cur(i) + 1) * per_block, n_halo_blocks - 1), 0)),
            pl.BlockSpec((GROUP // V7X_LANES, block, V7X_LANES), lambda i: (0, cur(i), 0)),
            _layer_spec(layer, p["gains"].shape[1:]),
            _layer_spec(layer, p["vec"].shape[1:]),
            _layer_spec(layer, p["mats"].shape[1:]),
            _layer_spec(layer, p["wcat"].shape[1:]),
            _layer_spec(layer, p["w_rows"].shape[1:]),
            _layer_spec(layer, p["wd"].shape[1:]),
        ],
        out_specs=pl.BlockSpec((block, D_MODEL), lambda i: (jnp.maximum(i - 1, 0), 0)),
        out_shape=jax.ShapeDtypeStruct(x2.shape, x2.dtype),
        scratch_shapes=[
            pltpu.VMEM((tile + 2 * HALO, GROUP), F32),
            pltpu.VMEM((tile + 2 * HALO + V7X_SUBLANES, GROUP), F32),
            pltpu.VMEM((2, tile + 2 * HALO, V7X_LANES), F32),
            pltpu.VMEM((tile, D_FF), BF16),
            pltpu.VMEM((MIXER_SUB, tile, D_MODEL), F32),
            pltpu.VMEM((MIXER_SUB, tile, D_MODEL), BF16),
        ],
        compiler_params=pltpu.CompilerParams(
            dimension_semantics=("arbitrary",),
            vmem_limit_bytes=V7X_VMEM_LIMIT_BYTES),
        name="mixer_ffn",
    )(x2, x2, x2, spec, p["gains"], p["vec"], p["mats"], p["wcat"], p["w_rows"], p["wd"])


def _block_diag(blocks):
    depth, n, k, _ = blocks.shape
    on_diagonal = jnp.eye(n, dtype=bool)[None, :, None, :, None]
    return jnp.where(on_diagonal, blocks[:, :, :, None, :], 0.0).reshape(depth, n * k, n * k)


def _pack_params(pre_mix_gain, post_mix_gain, pre_ffn_gain, post_ffn_gain, w_in, conv_w, pool_w, pool_scale,
                 fourier_w, spatial_w, spatial_b, group_norm_gain, w_out, w_gate, w_up, w_down):
    depth = w_in.shape[0]
    sl = lambda k: w_in[:, :, k * GROUP:(k + 1) * GROUP]
    w_rows = jnp.concatenate([sl(1), sl(2), sl(3), sl(0), sl(5), sl(6), sl(4), w_out, w_gate, w_up],
                             axis=2).astype(BF16)
    assert w_rows.shape[2] == W_ROWS_COLS
    mavg = np.kron(np.eye(N_HEADS), np.full((HEAD_DIM, HEAD_DIM), 1.0 / HEAD_DIM))
    mavg = jnp.broadcast_to(jnp.asarray(mavg, F32), (depth, GROUP, GROUP))
    sbias = jnp.repeat(jnp.transpose(spatial_b, (0, 2, 1)), HEAD_DIM, axis=2)
    row = lambda g: g[:, None, :]
    return dict(
        gains=jnp.concatenate([row(pre_mix_gain), row(post_mix_gain), row(pre_ffn_gain), row(post_ffn_gain),
                               row(group_norm_gain), jnp.zeros((depth, V7X_SUBLANES - 5, D_MODEL), F32)],
                              axis=1),
        vec=jnp.concatenate([sbias, conv_w, row(pool_scale),
                             jnp.zeros((depth, V7X_SUBLANES - 4, GROUP), F32)], axis=1),
        mats=jnp.stack([_block_diag(pool_w), _block_diag(fourier_w), mavg], axis=1).astype(BF16),
        wcat=jnp.transpose(spatial_w, (0, 2, 1, 3)).reshape(depth, CHUNK, N_HEADS * CHUNK).astype(BF16),
        w_rows=w_rows,
        wd=w_down.astype(BF16),
    )


def kernel(x, pre_mix_gain, post_mix_gain, pre_ffn_gain, post_ffn_gain, w_in, conv_w, pool_w, pool_scale,
           fourier_w, spatial_w, spatial_b, group_norm_gain, w_out, w_gate, w_up, w_down):
    b, seq, d = x.shape
    assert d == D_MODEL and seq % SEQ_LO == 0 and seq % (MIXER_SUB * min(TOKEN_TILE, seq)) == 0
    tables = _fourier_tables(seq)
    p = _pack_params(pre_mix_gain, post_mix_gain, pre_ffn_gain, post_ffn_gain, w_in, conv_w, pool_w,
                     pool_scale, fourier_w, spatial_w, spatial_b, group_norm_gain, w_out, w_gate, w_up,
                     w_down)
    for layer in range(w_in.shape[0]):
        spec = _fourier_spec(x, layer, p, tables)
        x = _mixer_ffn(x.reshape(b * seq, d), spec, seq, layer, p).reshape(b, seq, d)
    return x
```

```python
import functools

import jax
import jax.numpy as jnp
import numpy as np
from jax import lax
from jax.experimental import pallas as pl
from jax.experimental.pallas import tpu as pltpu

D_MODEL = 1024
GROUP = 256
HEAD_DIM = 64
N_HEADS = GROUP // HEAD_DIM
CHUNK = 128
D_FF = 2816
POOL_WINDOWS = (2, 4, 8, 16)
EPS = 1e-6

V7X_LANES = 128
V7X_SUBLANES = 8
V7X_MXU_COLS = 256
V7X_VMEM_LIMIT_BYTES = 56 * 1024 * 1024

SEQ_LO = 128
HALO = 16
TOKEN_TILE = 512
MIXER_SUB = 1
FF_CHUNK = V7X_MXU_COLS
W_IN_COL = 0
W_F_COL = W_IN_COL + 6 * GROUP
W_OUT_COL = W_F_COL + GROUP
W_GATE_COL = W_OUT_COL + D_MODEL
W_UP_COL = W_GATE_COL + D_FF
W_ROWS_COLS = W_UP_COL + D_FF
FOURIER_SUB = 4

F32 = jnp.float32
BF16 = jnp.bfloat16


def _dot(a, b):
    return jnp.dot(a, b, preferred_element_type=F32)


def _rms(x, g):
    ms = jnp.mean(x * x, axis=-1, keepdims=True)
    return x * lax.rsqrt(ms + EPS) * g


def _fourier_tables(seq):
    seq_hi = seq // SEQ_LO
    cm = np.outer(np.arange(HEAD_DIM), np.arange(HEAD_DIM)) % HEAD_DIM
    ang = 2.0 * np.pi * cm / HEAD_DIM
    cos_bd = np.kron(np.eye(N_HEADS), np.cos(ang)) / np.sqrt(HEAD_DIM)
    sin_bd = np.kron(np.eye(N_HEADS), np.sin(ang)) / np.sqrt(HEAD_DIM)
    fc = np.block([[cos_bd, -sin_bd], [sin_bd, cos_bd]])
    kt = np.outer(np.arange(seq_hi), np.arange(seq_hi)) % seq_hi
    ang1 = 2.0 * np.pi * kt / seq_hi
    eye = np.eye(V7X_SUBLANES)
    c1 = np.kron(np.cos(ang1) / np.sqrt(seq_hi), eye)
    s1 = np.kron(np.sin(ang1) / np.sqrt(seq_hi), eye)
    kd = np.concatenate([c1, -s1], axis=0)
    k = np.arange(SEQ_LO)[None, :, None] * seq_hi + np.arange(seq_hi)[:, None, None]
    ang2 = 2.0 * np.pi * ((k * np.arange(SEQ_LO)[None, None, :]) % seq) / seq
    e = np.concatenate([np.cos(ang2), np.sin(ang2)], axis=2) / np.sqrt(SEQ_LO)
    return (jnp.asarray(fc, F32).astype(BF16), jnp.asarray(kd, F32).astype(BF16),
            jnp.asarray(e, F32).astype(BF16))


def _fourier_kernel(xa_ref, xb_ref, g_ref, wf_ref, fc_ref, kd_ref, e_ref, o_ref, y_s, f_s, *, seq_hi, n_tb):
    step = pl.program_id(0)
    tb = jnp.maximum(step - 1, 0) % n_tb
    rows = seq_hi * V7X_SUBLANES
    t_lo_per_step = V7X_SUBLANES * FOURIER_SUB

    @pl.when(step == 0)
    def _():
        f_s[...] = jnp.zeros_like(f_s)

    def stage_norm(q):
        js = slice(q * V7X_SUBLANES, (q + 1) * V7X_SUBLANES)
        x = jnp.concatenate([xa_ref[0, :, js, :], xb_ref[0, :, js, :]], axis=0).reshape(rows, D_MODEL)
        inv = lax.rsqrt(jnp.mean(x * x, axis=-1, keepdims=True) + EPS)
        return (x * g_ref[0:1, :]).astype(BF16), inv

    def stage_proj(q, h_inv):
        h, inv = h_inv
        return (_dot(h, wf_ref[...]) * inv).astype(BF16)

    def stage_slow_dft(q, f):
        u = _dot(kd_ref[...], f)
        return jnp.concatenate([u[:rows], u[rows:]], axis=1).astype(BF16)

    def stage_channel_dft(q, uc):
        return _dot(uc, fc_ref[...])

    vals = []
    for q in range(FOURIER_SUB):
        h_inv = stage_norm(q)
        uc = stage_slow_dft(q, f_s[q])
        f_s[q] = stage_proj(q, h_inv)
        vals.append(stage_channel_dft(q, uc))

    row = pl.multiple_of(tb * t_lo_per_step, t_lo_per_step)
    for k in range(seq_hi):
        rs = slice(k * V7X_SUBLANES, (k + 1) * V7X_SUBLANES)
        ls = slice(k * GROUP, (k + 1) * GROUP)
        piece = jnp.concatenate([v[rs] for v in vals], axis=0).astype(BF16)
        y_s[pl.ds(row, t_lo_per_step), ls] = piece[:, :GROUP]
        y_s[pl.ds(SEQ_LO + row, t_lo_per_step), ls] = piece[:, GROUP:]

    @pl.when((step > 0) & (tb == n_tb - 1))
    def _():
        def out_block(kb, carry):
            for j in range(V7X_SUBLANES):
                ls = pl.ds(pl.multiple_of((kb * V7X_SUBLANES + j) * GROUP, GROUP), GROUP)
                o_ref[0, :, ls] = _dot(e_ref[kb * V7X_SUBLANES + j], y_s[:, ls])
            return carry

        lax.fori_loop(0, seq_hi // V7X_SUBLANES, out_block, 0)


def _const_spec(shape):
    n = len(shape)
    return pl.BlockSpec(shape, lambda *_: (0,) * n, pipeline_mode=pl.Buffered(1))


def _layer_spec(layer, shape, block_index=None):
    index = (0,) * len(shape) if block_index is None else block_index
    return pl.BlockSpec((None,) + shape, lambda *_: (layer,) + index, pipeline_mode=pl.Buffered(1))


def _fourier_spec(x, layer, params, tables):
    b, seq, _ = x.shape
    seq_hi = seq // SEQ_LO
    t_lo_per_step = V7X_SUBLANES * FOURIER_SUB
    n_tb = SEQ_LO // t_lo_per_step
    rows = seq_hi * V7X_SUBLANES
    fc, kd, e = tables
    x4 = x.reshape(b, seq_hi, SEQ_LO, D_MODEL)
    n_blocks = b * n_tb
    proj_block = lambda s: jnp.minimum(s, n_blocks - 1)
    dft_batch = lambda s: jnp.maximum(s - 1, 0) // n_tb
    out = pl.pallas_call(
        functools.partial(_fourier_kernel, seq_hi=seq_hi, n_tb=n_tb),
        grid=(n_blocks + 1,),
        in_specs=[
            pl.BlockSpec((1, seq_hi // 2, t_lo_per_step, D_MODEL),
                         lambda s: (proj_block(s) // n_tb, 0, proj_block(s) % n_tb, 0)),
            pl.BlockSpec((1, seq_hi // 2, t_lo_per_step, D_MODEL),
                         lambda s: (proj_block(s) // n_tb, 1, proj_block(s) % n_tb, 0)),
            _layer_spec(layer, (V7X_SUBLANES, D_MODEL)),
            _layer_spec(layer, (D_MODEL, GROUP), (0, W_F_COL // GROUP)),
            _const_spec((2 * GROUP, 2 * GROUP)),
            _const_spec((2 * rows, rows)),
            _const_spec((seq_hi, SEQ_LO, 2 * SEQ_LO)),
        ],
        out_specs=pl.BlockSpec((1, SEQ_LO, seq_hi * GROUP), lambda s: (dft_batch(s), 0, 0)),
        out_shape=jax.ShapeDtypeStruct((b, SEQ_LO, seq_hi * GROUP), F32),
        scratch_shapes=[
            pltpu.VMEM((2 * SEQ_LO, seq_hi * GROUP), BF16),
            pltpu.VMEM((FOURIER_SUB, rows, GROUP), BF16),
        ],
        compiler_params=pltpu.CompilerParams(
            dimension_semantics=("arbitrary",),
            vmem_limit_bytes=V7X_VMEM_LIMIT_BYTES),
        name="fourier_spec",
    )(x4, x4, params["gains"], params["w_rows"], fc, kd, e)
    return out


def _spec_in_sequence_order(spec_ref, sp_s, tile, seq_hi):
    n_a, n_c = seq_hi // V7X_SUBLANES, tile // seq_hi // V7X_SUBLANES
    halves = []
    for half in range(GROUP // V7X_LANES):
        for c in range(n_c):
            for k_lo in range(seq_hi):
                a, j = divmod(k_lo, V7X_SUBLANES)
                lanes = slice(k_lo * GROUP + half * V7X_LANES, k_lo * GROUP + (half + 1) * V7X_LANES)
                start = (c * n_a + a) * V7X_SUBLANES * V7X_SUBLANES + j
                sp_s[half, pl.ds(start, V7X_SUBLANES, stride=V7X_SUBLANES), :] = (
                    spec_ref[c * V7X_SUBLANES:(c + 1) * V7X_SUBLANES, lanes])
        halves.append(jnp.concatenate(
            [sp_s[half, pl.ds(((c * n_a + a) * V7X_SUBLANES + i) * V7X_SUBLANES, V7X_SUBLANES), :]
             for c in range(n_c) for i in range(V7X_SUBLANES) for a in range(n_a)], axis=0))
    return jnp.concatenate(halves, axis=1)


def _mixer_ffn_kernel(x_ref, xp_ref, xn_ref, spec_ref, gains_ref, vec_ref, mats_ref, wcat_ref, w_rows_ref,
                      wd_ref,
                      o_ref, cz_s, p_s, ps_s, sp_s, act_s, x1_all_s, h2_all_s, *, tile, seq, n_blocks):
    g_pre_ref, g_post_ref, g_pre_ffn_ref, g_post_ffn_ref, g_grp_ref = (
        gains_ref.at[k:k + 1] for k in range(5))
    sbias_ref = vec_ref.at[0:CHUNK]
    conv_w_ref = vec_ref.at[CHUNK:CHUNK + 3]
    pool_scale_ref = vec_ref.at[CHUNK + 3:CHUNK + 4]
    pool_w_ref, four_w_ref, mavg_ref = (mats_ref.at[k] for k in range(3))
    w_in_ref = w_rows_ref.at[:, W_IN_COL:W_IN_COL + 6 * GROUP]
    w_out_ref = w_rows_ref.at[:, W_OUT_COL:W_OUT_COL + D_MODEL]
    wg_ref = w_rows_ref.at[:, W_GATE_COL:W_GATE_COL + D_FF]
    wu_ref = w_rows_ref.at[:, W_UP_COL:W_UP_COL + D_FF]
    step = pl.program_id(0)

    @pl.when(step == 0)
    def _():
        x1_all_s[...] = jnp.zeros_like(x1_all_s)
        h2_all_s[...] = jnp.zeros_like(h2_all_s)

    blk = jnp.minimum(step, n_blocks - 1)
    tiles_per_seq = seq // tile
    k_hi_per_tile = tile // (seq // SEQ_LO)
    for k in range(MIXER_SUB):
        rows_k = pl.ds(k * tile, tile)
        prev_ref = xp_ref if k == 0 else x_ref.at[pl.ds(k * tile - HALO, HALO)]
        next_ref = xn_ref if k == MIXER_SUB - 1 else x_ref.at[pl.ds((k + 1) * tile, HALO)]
        _mixer_ffn_tile(
            x_ref.at[rows_k], prev_ref, next_ref, spec_ref.at[0, pl.ds(k * k_hi_per_tile, k_hi_per_tile)],
            o_ref.at[rows_k],
            x1_all_s.at[k], h2_all_s.at[k], (blk * MIXER_SUB + k) % tiles_per_seq, tiles_per_seq,
            g_pre_ref, g_post_ref, g_pre_ffn_ref, g_post_ffn_ref, g_grp_ref, sbias_ref, conv_w_ref,
            pool_scale_ref, pool_w_ref, four_w_ref, mavg_ref, wcat_ref, w_in_ref, w_out_ref, wg_ref, wu_ref,
            wd_ref, cz_s, p_s, ps_s, sp_s, act_s, tile=tile, seq=seq)


def _mixer_ffn_tile(x_ref, xp_ref, xn_ref, spec_ref, o_ref, x1_s, h2_s, ti, tiles_per_seq,
                    g_pre_ref, g_post_ref, g_pre_ffn_ref, g_post_ffn_ref, g_grp_ref, sbias_ref, conv_w_ref,
                    pool_scale_ref, pool_w_ref, four_w_ref, mavg_ref, wcat_ref, w_in_ref, w_out_ref, wg_ref,
                    wu_ref, wd_ref, cz_s, p_s, ps_s, sp_s, act_s, *, tile, seq):
    def ffn_chunks(lo, hi):
        for j in range(lo, hi):
            ls = slice(j * FF_CHUNK, (j + 1) * FF_CHUNK)
            gate = _dot(h2_s[...], wg_ref[:, ls])
            up = _dot(h2_s[...], wu_ref[:, ls])
            act_s[:, ls] = (gate * jax.nn.sigmoid(gate) * up).astype(BF16)

    def ffn_down(rows):
        f = _dot(act_s[rows, :], wd_ref[...])
        o_ref[rows, :] = o_ref[rows, :] + _rms(f, g_post_ffn_ref[...])

    o_ref[...] = x1_s[...]
    ffn_chunks(0, 1)

    x = x_ref[...]
    xh = jnp.concatenate([xp_ref[...], xn_ref[...]], axis=0)

    hm = _rms(x, g_pre_ref[...]).astype(BF16)
    hh = _rms(xh, g_pre_ref[...]).astype(BF16)
    ffn_chunks(1, 2)
    pe = _dot(jnp.concatenate([hm, hh], axis=0), w_in_ref[:, :3 * GROUP])
    pm = jnp.concatenate([pe[:tile], _dot(hm, w_in_ref[:, 3 * GROUP:])], axis=1)
    ph = pe[tile:]
    ffn_chunks(2, 3)
    hrow = lax.broadcasted_iota(jnp.int32, (2 * HALO, 1), 0)
    has_prev = (ti != 0).astype(jnp.int32)
    has_next = (ti != tiles_per_seq - 1).astype(jnp.int32)
    ph = jnp.where(jnp.where(hrow < HALO, has_prev, has_next) != 0, ph, 0.0)

    cz_s[pl.ds(0, HALO), :] = ph[:HALO, :GROUP] * ph[:HALO, GROUP:2 * GROUP]
    cz_s[pl.ds(HALO, tile), :] = pm[:, :GROUP] * pm[:, GROUP:2 * GROUP]
    cz_s[pl.ds(HALO + tile, HALO), :] = ph[HALO:, :GROUP] * ph[HALO:, GROUP:2 * GROUP]
    p_s[pl.ds(0, HALO), :] = ph[:HALO, 2 * GROUP:]
    p_s[pl.ds(HALO, tile), :] = pm[:, 2 * GROUP:3 * GROUP]
    p_s[pl.ds(HALO + tile, HALO), :] = ph[HALO:, 2 * GROUP:]
    p_s[pl.ds(2 * HALO + tile, V7X_SUBLANES), :] = jnp.zeros((V7X_SUBLANES, GROUP), F32)
    b_gate = pm[:, 3 * GROUP:4 * GROUP]
    u = pm[:, 4 * GROUP:5 * GROUP]
    v = pm[:, 5 * GROUP:]

    y_conv = b_gate * (conv_w_ref[0:1, :] * cz_s[pl.ds(HALO - 1, tile), :]
                       + conv_w_ref[1:2, :] * cz_s[pl.ds(HALO, tile), :]
                       + conv_w_ref[2:3, :] * cz_s[pl.ds(HALO + 1, tile), :])
    ffn_chunks(3, 4)

    pos = ti * tile + lax.broadcasted_iota(jnp.int32, (tile, 1), 0)
    lane = lax.broadcasted_iota(jnp.int32, (1, V7X_LANES), 1)
    first_half = lane < HEAD_DIM
    def pool_col(col, inner, outer):
        w_small, w_big = POOL_WINDOWS[2 * col], POOL_WINDOWS[2 * col + 1]
        half = jnp.where(first_half, w_small // 2, w_big // 2)
        cnt = (jnp.minimum(pos + half, seq) - jnp.maximum(pos - half, 0)).astype(F32)
        mean = jnp.where(first_half, inner, outer) / cnt
        return mean - p_s[pl.ds(HALO, tile), col * V7X_LANES:(col + 1) * V7X_LANES]

    ls0 = slice(0, V7X_LANES)
    inner0 = p_s[pl.ds(HALO - 1, tile), ls0] + p_s[pl.ds(HALO, tile), ls0]
    outer0 = inner0 + p_s[pl.ds(HALO - 2, tile), ls0] + p_s[pl.ds(HALO + 1, tile), ls0]
    d_col0 = pool_col(0, inner0, outer0)
    ffn_chunks(4, 5)
    ls1 = slice(V7X_LANES, 2 * V7X_LANES)
    base = HALO - POOL_WINDOWS[3] // 2
    n2, n4, n8 = tile + 24, tile + 16, tile + 8
    ps_s[0, pl.ds(base, n2), :] = p_s[pl.ds(base, n2), ls1] + p_s[pl.ds(base + 1, n2), ls1]
    ps_s[1, pl.ds(base, n4), :] = ps_s[0, pl.ds(base, n4), :] + ps_s[0, pl.ds(base + 2, n4), :]
    ps_s[0, pl.ds(base, n8), :] = ps_s[1, pl.ds(base, n8), :] + ps_s[1, pl.ds(base + 4, n8), :]
    inner1 = ps_s[0, pl.ds(HALO - 4, tile), :]
    outer1 = ps_s[0, pl.ds(HALO - 8, tile), :] + ps_s[0, pl.ds(HALO, tile), :]
    d = jnp.concatenate([d_col0, pool_col(1, inner1, outer1)], axis=1).astype(BF16)
    y_pool = _dot(d, pool_w_ref[...]) * pool_scale_ref[...]

    spec = _spec_in_sequence_order(spec_ref, sp_s, tile, seq // SEQ_LO).astype(BF16)
    y_four = _dot(spec, four_w_ref[...])
    ffn_chunks(5, 6)

    mu = _dot(v.astype(BF16), mavg_ref[...])
    vc = v - mu
    var = _dot((vc * vc).astype(BF16), mavg_ref[...])
    vh = vc * lax.rsqrt(var + EPS)
    ffn_chunks(6, 7)
    lane_g = lax.broadcasted_iota(jnp.int32, (1, GROUP), 1)
    s_chunks = []
    for n in range(tile // CHUNK):
        vn = vh[n * CHUNK:(n + 1) * CHUNK]
        stack = jnp.concatenate(
            [jnp.where(lane_g // HEAD_DIM == hd, vn, 0.0).astype(BF16) for hd in range(N_HEADS)], axis=0)
        s_chunks.append(_dot(wcat_ref[...], stack) + sbias_ref[...])
    y_gmlp = u * jnp.concatenate(s_chunks, axis=0)
    ffn_chunks(7, 8)

    y = jnp.concatenate(
        [_rms(yk, g_grp_ref[:, k * GROUP:(k + 1) * GROUP]).astype(BF16)
         for k, yk in enumerate((y_conv, y_pool, y_four, y_gmlp))], axis=1)
    ffn_chunks(8, 9)

    def out_half(rows):
        m = _dot(y[rows, :], w_out_ref[...])
        x1 = x_ref[rows, :] + _rms(m, g_post_ref[...])
        return x1, _rms(x1, g_pre_ffn_ref[...]).astype(BF16)

    top, bot = slice(0, tile // 2), slice(tile // 2, tile)
    x1_top, h2_top = out_half(top)
    ffn_chunks(9, 10)
    x1_bot, h2_bot = out_half(bot)
    ffn_chunks(10, D_FF // FF_CHUNK)
    ffn_down(top)
    x1_s[top, :] = x1_top
    h2_s[top, :] = h2_top
    x1_s[bot, :] = x1_bot
    h2_s[bot, :] = h2_bot
    ffn_down(bot)


def _mixer_ffn(x2, spec, seq, layer, p):
    n_tok = x2.shape[0]
    tile = min(TOKEN_TILE, seq)
    block = MIXER_SUB * tile
    n_blocks = n_tok // block
    n_halo_blocks = n_tok // HALO
    per_block = block // HALO
    seq_hi = seq // SEQ_LO
    k_hi_per_block = block // seq_hi
    blocks_per_seq = seq // block
    cur = lambda i: jnp.minimum(i, n_blocks - 1)
    return pl.pallas_call(
        functools.partial(_mixer_ffn_kernel, tile=tile, seq=seq, n_blocks=n_blocks),
        grid=(n_blocks + 1,),
        in_specs=[
            pl.BlockSpec((block, D_MODEL), lambda i: (cur(i), 0)),
            pl.BlockSpec((HALO, D_MODEL), lambda i: (jnp.maximum(cur(i) * per_block - 1, 0), 0)),
            pl.BlockSpec((HALO, D_MODEL),
                         lambda i: (jnp.minimum((cur(i) + 1) * per_block, n_halo_blocks - 1), 0)),
            pl.BlockSpec((1, k_hi_per_block, seq_hi * GROUP),
                         lambda i: (cur(i) // blocks_per_seq, cur(i) % blocks_per_seq, 0)),
            _layer_spec(layer, p["gains"].shape[1:]),
            _layer_spec(layer, p["vec"].shape[1:]),
            _layer_spec(layer, p["mats"].shape[1:]),
            _layer_spec(layer, p["wcat"].shape[1:]),
            _layer_spec(layer, p["w_rows"].shape[1:]),
            _layer_spec(layer, p["wd"].shape[1:]),
        ],
        out_specs=pl.BlockSpec((block, D_MODEL), lambda i: (jnp.maximum(i - 1, 0), 0)),
        out_shape=jax.ShapeDtypeStruct(x2.shape, x2.dtype),
        scratch_shapes=[
            pltpu.VMEM((tile + 2 * HALO, GROUP), F32),
            pltpu.VMEM((tile + 2 * HALO + V7X_SUBLANES, GROUP), F32),
            pltpu.VMEM((2, tile + 2 * HALO, V7X_LANES), F32),
            pltpu.VMEM((GROUP // V7X_LANES, tile, V7X_LANES), F32),
            pltpu.VMEM((tile, D_FF), BF16),
            pltpu.VMEM((MIXER_SUB, tile, D_MODEL), F32),
            pltpu.VMEM((MIXER_SUB, tile, D_MODEL), BF16),
        ],
        compiler_params=pltpu.CompilerParams(
            dimension_semantics=("arbitrary",),
            vmem_limit_bytes=V7X_VMEM_LIMIT_BYTES),
        name="mixer_ffn",
    )(x2, x2, x2, spec, p["gains"], p["vec"], p["mats"], p["wcat"], p["w_rows"], p["wd"])


def _block_diag(blocks):
    depth, n, k, _ = blocks.shape
    on_diagonal = jnp.eye(n, dtype=bool)[None, :, None, :, None]
    return jnp.where(on_diagonal, blocks[:, :, :, None, :], 0.0).reshape(depth, n * k, n * k)


def _pack_params(pre_mix_gain, post_mix_gain, pre_ffn_gain, post_ffn_gain, w_in, conv_w, pool_w, pool_scale,
                 fourier_w, spatial_w, spatial_b, group_norm_gain, w_out, w_gate, w_up, w_down):
    depth = w_in.shape[0]
    sl = lambda k: w_in[:, :, k * GROUP:(k + 1) * GROUP]
    w_rows = jnp.concatenate([sl(1), sl(2), sl(3), sl(0), sl(5), sl(6), sl(4), w_out, w_gate, w_up],
                             axis=2).astype(BF16)
    assert w_rows.shape[2] == W_ROWS_COLS
    mavg = np.kron(np.eye(N_HEADS), np.full((HEAD_DIM, HEAD_DIM), 1.0 / HEAD_DIM))
    mavg = jnp.broadcast_to(jnp.asarray(mavg, F32), (depth, GROUP, GROUP))
    sbias = jnp.repeat(jnp.transpose(spatial_b, (0, 2, 1)), HEAD_DIM, axis=2)
    row = lambda g: g[:, None, :]
    return dict(
        gains=jnp.concatenate([row(pre_mix_gain), row(post_mix_gain), row(pre_ffn_gain), row(post_ffn_gain),
                               row(group_norm_gain), jnp.zeros((depth, V7X_SUBLANES - 5, D_MODEL), F32)],
                              axis=1),
        vec=jnp.concatenate([sbias, conv_w, row(pool_scale),
                             jnp.zeros((depth, V7X_SUBLANES - 4, GROUP), F32)], axis=1),
        mats=jnp.stack([_block_diag(pool_w), _block_diag(fourier_w), mavg], axis=1).astype(BF16),
        wcat=jnp.transpose(spatial_w, (0, 2, 1, 3)).reshape(depth, CHUNK, N_HEADS * CHUNK).astype(BF16),
        w_rows=w_rows,
        wd=w_down.astype(BF16),
    )


def kernel(x, pre_mix_gain, post_mix_gain, pre_ffn_gain, post_ffn_gain, w_in, conv_w, pool_w, pool_scale,
           fourier_w, spatial_w, spatial_b, group_norm_gain, w_out, w_gate, w_up, w_down):
    b, seq, d = x.shape
    assert d == D_MODEL and seq % SEQ_LO == 0 and seq % (MIXER_SUB * min(TOKEN_TILE, seq)) == 0
    tables = _fourier_tables(seq)
    p = _pack_params(pre_mix_gain, post_mix_gain, pre_ffn_gain, post_ffn_gain, w_in, conv_w, pool_w,
                     pool_scale, fourier_w, spatial_w, spatial_b, group_norm_gain, w_out, w_gate, w_up,
                     w_down)
    for layer in range(w_in.shape[0]):
        spec = _fourier_spec(x, layer, p, tables)
        x = _mixer_ffn(x.reshape(b * seq, d), spec, seq, layer, p).reshape(b, seq, d)
    return x
```

```python
import functools

import jax
import jax.numpy as jnp
import numpy as np
from jax import lax
from jax.experimental import pallas as pl
from jax.experimental.pallas import tpu as pltpu

D_MODEL = 1024
GROUP = 256
HEAD_DIM = 64
N_HEADS = GROUP // HEAD_DIM
CHUNK = 128
D_FF = 2816
POOL_WINDOWS = (2, 4, 8, 16)
EPS = 1e-6

V7X_LANES = 128
V7X_SUBLANES = 8
V7X_MXU_COLS = 256
V7X_VMEM_LIMIT_BYTES = 56 * 1024 * 1024

SEQ_LO = 128
HALO = 16
TOKEN_TILE = 512
MIXER_SUB = 1
FF_CHUNK = V7X_MXU_COLS
W_IN_COL = 0
W_F_COL = W_IN_COL + 6 * GROUP
W_OUT_COL = W_F_COL + GROUP
W_GATE_COL = W_OUT_COL + D_MODEL
W_UP_COL = W_GATE_COL + D_FF
W_ROWS_COLS = W_UP_COL + D_FF
FOURIER_SUB = 4

F32 = jnp.float32
BF16 = jnp.bfloat16


def _dot(a, b):
    return jnp.dot(a, b, preferred_element_type=F32)


def _rms(x, g):
    ms = jnp.mean(x * x, axis=-1, keepdims=True)
    return x * lax.rsqrt(ms + EPS) * g


def _fourier_tables(seq):
    seq_hi = seq // SEQ_LO
    cm = np.outer(np.arange(HEAD_DIM), np.arange(HEAD_DIM)) % HEAD_DIM
    ang = 2.0 * np.pi * cm / HEAD_DIM
    cos_bd = np.kron(np.eye(N_HEADS), np.cos(ang)) / np.sqrt(HEAD_DIM)
    sin_bd = np.kron(np.eye(N_HEADS), np.sin(ang)) / np.sqrt(HEAD_DIM)
    fc = np.block([[cos_bd, -sin_bd], [sin_bd, cos_bd]])
    kt = np.outer(np.arange(seq_hi), np.arange(seq_hi)) % seq_hi
    ang1 = 2.0 * np.pi * kt / seq_hi
    eye = np.eye(V7X_SUBLANES)
    c1 = np.kron(np.cos(ang1) / np.sqrt(seq_hi), eye)
    s1 = np.kron(np.sin(ang1) / np.sqrt(seq_hi), eye)
    kd = np.concatenate([c1, -s1], axis=0)
    k = np.arange(SEQ_LO)[None, :, None] * seq_hi + np.arange(seq_hi)[:, None, None]
    ang2 = 2.0 * np.pi * ((k * np.arange(SEQ_LO)[None, None, :]) % seq) / seq
    e = np.concatenate([np.cos(ang2), np.sin(ang2)], axis=2) / np.sqrt(SEQ_LO)
    return (jnp.asarray(fc, F32).astype(BF16), jnp.asarray(kd, F32).astype(BF16),
            jnp.asarray(e, F32).astype(BF16))


def _fourier_kernel(x_ref, g_ref, wf_ref, fc_ref, kd_ref, e_ref, o_ref, y_s, f_s, *, seq_hi, n_tb):
    step = pl.program_id(0)
    tb = jnp.maximum(step - 1, 0) % n_tb
    rows = seq_hi * V7X_SUBLANES
    t_lo_per_step = V7X_SUBLANES * FOURIER_SUB

    @pl.when(step == 0)
    def _():
        f_s[...] = jnp.zeros_like(f_s)

    def stage_norm(q):
        x = x_ref[0, :, q * V7X_SUBLANES:(q + 1) * V7X_SUBLANES, :].reshape(rows, D_MODEL)
        inv = lax.rsqrt(jnp.mean(x * x, axis=-1, keepdims=True) + EPS)
        return (x * g_ref[0:1, :]).astype(BF16), inv

    def stage_proj(q, h_inv):
        h, inv = h_inv
        return (_dot(h, wf_ref[...]) * inv).astype(BF16)

    def stage_slow_dft(q, f):
        u = _dot(kd_ref[...], f)
        return jnp.concatenate([u[:rows], u[rows:]], axis=1).astype(BF16)

    def stage_channel_dft(q, uc):
        return _dot(uc, fc_ref[...])

    vals = []
    for q in range(FOURIER_SUB):
        h_inv = stage_norm(q)
        uc = stage_slow_dft(q, f_s[q])
        f_s[q] = stage_proj(q, h_inv)
        vals.append(stage_channel_dft(q, uc))

    row = pl.multiple_of(tb * t_lo_per_step, t_lo_per_step)
    for k in range(seq_hi):
        rs = slice(k * V7X_SUBLANES, (k + 1) * V7X_SUBLANES)
        ls = slice(k * GROUP, (k + 1) * GROUP)
        piece = jnp.concatenate([v[rs] for v in vals], axis=0).astype(BF16)
        y_s[pl.ds(row, t_lo_per_step), ls] = piece[:, :GROUP]
        y_s[pl.ds(SEQ_LO + row, t_lo_per_step), ls] = piece[:, GROUP:]

    @pl.when((step > 0) & (tb == n_tb - 1))
    def _():
        def out_block(kb, carry):
            for j in range(V7X_SUBLANES):
                k_lo = kb * V7X_SUBLANES + j
                res = _dot(e_ref[k_lo], y_s[:, pl.ds(pl.multiple_of(k_lo * GROUP, GROUP), GROUP)])
                for half in range(GROUP // V7X_LANES):
                    ls = slice(half * V7X_LANES, (half + 1) * V7X_LANES)
                    o_ref[half, 0, kb, pl.ds(j, SEQ_LO, stride=V7X_SUBLANES), :] = res[:, ls]
            return carry

        lax.fori_loop(0, seq_hi // V7X_SUBLANES, out_block, 0)


def _const_spec(shape):
    n = len(shape)
    return pl.BlockSpec(shape, lambda *_: (0,) * n, pipeline_mode=pl.Buffered(1))


def _layer_spec(layer, shape, block_index=None):
    index = (0,) * len(shape) if block_index is None else block_index
    return pl.BlockSpec((None,) + shape, lambda *_: (layer,) + index, pipeline_mode=pl.Buffered(1))


def _fourier_spec(x, layer, params, tables):
    b, seq, _ = x.shape
    seq_hi = seq // SEQ_LO
    t_lo_per_step = V7X_SUBLANES * FOURIER_SUB
    n_tb = SEQ_LO // t_lo_per_step
    rows = seq_hi * V7X_SUBLANES
    fc, kd, e = tables
    halves = GROUP // V7X_LANES
    k_lo_blocks = seq_hi // V7X_SUBLANES
    out_block = (halves, 1, k_lo_blocks, SEQ_LO * V7X_SUBLANES, V7X_LANES)
    x4 = x.reshape(b, seq_hi, SEQ_LO, D_MODEL)
    n_blocks = b * n_tb
    proj_block = lambda s: jnp.minimum(s, n_blocks - 1)
    dft_batch = lambda s: jnp.maximum(s - 1, 0) // n_tb
    out = pl.pallas_call(
        functools.partial(_fourier_kernel, seq_hi=seq_hi, n_tb=n_tb),
        grid=(n_blocks + 1,),
        in_specs=[
            pl.BlockSpec((1, seq_hi, t_lo_per_step, D_MODEL),
                         lambda s: (proj_block(s) // n_tb, 0, proj_block(s) % n_tb, 0)),
            _layer_spec(layer, (V7X_SUBLANES, D_MODEL)),
            _layer_spec(layer, (D_MODEL, GROUP), (0, W_F_COL // GROUP)),
            _const_spec((2 * GROUP, 2 * GROUP)),
            _const_spec((2 * rows, rows)),
            _const_spec((seq_hi, SEQ_LO, 2 * SEQ_LO)),
        ],
        out_specs=pl.BlockSpec(out_block, lambda s: (0, dft_batch(s), 0, 0, 0)),
        out_shape=jax.ShapeDtypeStruct((halves, b) + out_block[2:], F32),
        scratch_shapes=[
            pltpu.VMEM((2 * SEQ_LO, seq_hi * GROUP), BF16),
            pltpu.VMEM((FOURIER_SUB, rows, GROUP), BF16),
        ],
        compiler_params=pltpu.CompilerParams(
            dimension_semantics=("arbitrary",),
            vmem_limit_bytes=V7X_VMEM_LIMIT_BYTES),
        name="fourier_spec",
    )(x4, params["gains"], params["w_rows"], fc, kd, e)
    return out


def _mixer_ffn_kernel(x_ref, xp_ref, xn_ref, spec_ref, gains_ref, vec_ref, mats_ref, wcat_ref, w_rows_ref,
                      wd_ref,
                      o_ref, cz_s, p_s, ps_s, act_s, x1_all_s, h2_all_s, *, tile, seq, n_blocks):
    g_pre_ref, g_post_ref, g_pre_ffn_ref, g_post_ffn_ref, g_grp_ref = (
        gains_ref.at[k:k + 1] for k in range(5))
    sbias_ref = vec_ref.at[0:CHUNK]
    conv_w_ref = vec_ref.at[CHUNK:CHUNK + 3]
    pool_scale_ref = vec_ref.at[CHUNK + 3:CHUNK + 4]
    pool_w_ref, four_w_ref, mavg_ref = (mats_ref.at[k] for k in range(3))
    w_in_ref = w_rows_ref.at[:, W_IN_COL:W_IN_COL + 6 * GROUP]
    w_out_ref = w_rows_ref.at[:, W_OUT_COL:W_OUT_COL + D_MODEL]
    wg_ref = w_rows_ref.at[:, W_GATE_COL:W_GATE_COL + D_FF]
    wu_ref = w_rows_ref.at[:, W_UP_COL:W_UP_COL + D_FF]
    step = pl.program_id(0)

    @pl.when(step == 0)
    def _():
        x1_all_s[...] = jnp.zeros_like(x1_all_s)
        h2_all_s[...] = jnp.zeros_like(h2_all_s)

    blk = jnp.minimum(step, n_blocks - 1)
    tiles_per_seq = seq // tile
    for k in range(MIXER_SUB):
        rows_k = pl.ds(k * tile, tile)
        prev_ref = xp_ref if k == 0 else x_ref.at[pl.ds(k * tile - HALO, HALO)]
        next_ref = xn_ref if k == MIXER_SUB - 1 else x_ref.at[pl.ds((k + 1) * tile, HALO)]
        _mixer_ffn_tile(
            x_ref.at[rows_k], prev_ref, next_ref, spec_ref.at[:, 0, :, 0, k], o_ref.at[rows_k],
            x1_all_s.at[k], h2_all_s.at[k], (blk * MIXER_SUB + k) % tiles_per_seq, tiles_per_seq,
            g_pre_ref, g_post_ref, g_pre_ffn_ref, g_post_ffn_ref, g_grp_ref, sbias_ref, conv_w_ref,
            pool_scale_ref, pool_w_ref, four_w_ref, mavg_ref, wcat_ref, w_in_ref, w_out_ref, wg_ref, wu_ref,
            wd_ref, cz_s, p_s, ps_s, act_s, tile=tile, seq=seq)


def _mixer_ffn_tile(x_ref, xp_ref, xn_ref, spec_ref, o_ref, x1_s, h2_s, ti, tiles_per_seq,
                    g_pre_ref, g_post_ref, g_pre_ffn_ref, g_post_ffn_ref, g_grp_ref, sbias_ref, conv_w_ref,
                    pool_scale_ref, pool_w_ref, four_w_ref, mavg_ref, wcat_ref, w_in_ref, w_out_ref, wg_ref,
                    wu_ref, wd_ref, cz_s, p_s, ps_s, act_s, *, tile, seq):
    def ffn_chunks(lo, hi):
        for j in range(lo, hi):
            ls = slice(j * FF_CHUNK, (j + 1) * FF_CHUNK)
            gate = _dot(h2_s[...], wg_ref[:, ls])
            up = _dot(h2_s[...], wu_ref[:, ls])
            act_s[:, ls] = (gate * jax.nn.sigmoid(gate) * up).astype(BF16)

    def ffn_down(rows):
        f = _dot(act_s[rows, :], wd_ref[...])
        o_ref[rows, :] = x1_s[rows, :] + _rms(f, g_post_ffn_ref[...])

    ffn_chunks(0, 1)

    x = x_ref[...]
    xh = jnp.concatenate([xp_ref[...], xn_ref[...]], axis=0)

    hm = _rms(x, g_pre_ref[...]).astype(BF16)
    hh = _rms(xh, g_pre_ref[...]).astype(BF16)
    ffn_chunks(1, 2)
    pe = _dot(jnp.concatenate([hm, hh], axis=0), w_in_ref[:, :3 * GROUP])
    pm = jnp.concatenate([pe[:tile], _dot(hm, w_in_ref[:, 3 * GROUP:])], axis=1)
    ph = pe[tile:]
    ffn_chunks(2, 3)
    hrow = lax.broadcasted_iota(jnp.int32, (2 * HALO, 1), 0)
    has_prev = (ti != 0).astype(jnp.int32)
    has_next = (ti != tiles_per_seq - 1).astype(jnp.int32)
    ph = jnp.where(jnp.where(hrow < HALO, has_prev, has_next) != 0, ph, 0.0)

    cz_s[pl.ds(0, HALO), :] = ph[:HALO, :GROUP] * ph[:HALO, GROUP:2 * GROUP]
    cz_s[pl.ds(HALO, tile), :] = pm[:, :GROUP] * pm[:, GROUP:2 * GROUP]
    cz_s[pl.ds(HALO + tile, HALO), :] = ph[HALO:, :GROUP] * ph[HALO:, GROUP:2 * GROUP]
    p_s[pl.ds(0, HALO), :] = ph[:HALO, 2 * GROUP:]
    p_s[pl.ds(HALO, tile), :] = pm[:, 2 * GROUP:3 * GROUP]
    p_s[pl.ds(HALO + tile, HALO), :] = ph[HALO:, 2 * GROUP:]
    p_s[pl.ds(2 * HALO + tile, V7X_SUBLANES), :] = jnp.zeros((V7X_SUBLANES, GROUP), F32)
    b_gate = pm[:, 3 * GROUP:4 * GROUP]
    u = pm[:, 4 * GROUP:5 * GROUP]
    v = pm[:, 5 * GROUP:]

    y_conv = b_gate * (conv_w_ref[0:1, :] * cz_s[pl.ds(HALO - 1, tile), :]
                       + conv_w_ref[1:2, :] * cz_s[pl.ds(HALO, tile), :]
                       + conv_w_ref[2:3, :] * cz_s[pl.ds(HALO + 1, tile), :])
    ffn_chunks(3, 4)

    pos = ti * tile + lax.broadcasted_iota(jnp.int32, (tile, 1), 0)
    lane = lax.broadcasted_iota(jnp.int32, (1, V7X_LANES), 1)
    first_half = lane < HEAD_DIM

    def pool_col(col, inner, outer):
        w_small, w_big = POOL_WINDOWS[2 * col], POOL_WINDOWS[2 * col + 1]
        half = jnp.where(first_half, w_small // 2, w_big // 2)
        cnt = (jnp.minimum(pos + half, seq) - jnp.maximum(pos - half, 0)).astype(F32)
        mean = jnp.where(first_half, inner, outer) / cnt
        return mean - p_s[pl.ds(HALO, tile), col * V7X_LANES:(col + 1) * V7X_LANES]

    ls0 = slice(0, V7X_LANES)
    inner0 = p_s[pl.ds(HALO - 1, tile), ls0] + p_s[pl.ds(HALO, tile), ls0]
    outer0 = inner0 + p_s[pl.ds(HALO - 2, tile), ls0] + p_s[pl.ds(HALO + 1, tile), ls0]
    d_col0 = pool_col(0, inner0, outer0)
    ffn_chunks(4, 5)
    ls1 = slice(V7X_LANES, 2 * V7X_LANES)
    base = HALO - POOL_WINDOWS[3] // 2
    n2, n4, n8 = tile + 24, tile + 16, tile + 8
    ps_s[0, pl.ds(base, n2), :] = p_s[pl.ds(base, n2), ls1] + p_s[pl.ds(base + 1, n2), ls1]
    ps_s[1, pl.ds(base, n4), :] = ps_s[0, pl.ds(base, n4), :] + ps_s[0, pl.ds(base + 2, n4), :]
    ps_s[0, pl.ds(base, n8), :] = ps_s[1, pl.ds(base, n8), :] + ps_s[1, pl.ds(base + 4, n8), :]
    inner1 = ps_s[0, pl.ds(HALO - 4, tile), :]
    outer1 = ps_s[0, pl.ds(HALO - 8, tile), :] + ps_s[0, pl.ds(HALO, tile), :]
    d = jnp.concatenate([d_col0, pool_col(1, inner1, outer1)], axis=1).astype(BF16)
    y_pool = _dot(d, pool_w_ref[...]) * pool_scale_ref[...]

    n_kb, n_kh = spec_ref.shape[1], spec_ref.shape[2] // V7X_SUBLANES
    spec = jnp.concatenate(
        [jnp.concatenate([spec_ref[half, kb, pl.ds(i * V7X_SUBLANES, V7X_SUBLANES), :]
                          for i in range(n_kh) for kb in range(n_kb)], axis=0)
         for half in range(GROUP // V7X_LANES)], axis=1).astype(BF16)
    y_four = _dot(spec, four_w_ref[...])
    ffn_chunks(5, 6)

    mu = _dot(v.astype(BF16), mavg_ref[...])
    vc = v - mu
    var = _dot((vc * vc).astype(BF16), mavg_ref[...])
    vh = vc * lax.rsqrt(var + EPS)
    ffn_chunks(6, 7)
    lane_g = lax.broadcasted_iota(jnp.int32, (1, GROUP), 1)
    s_chunks = []
    for n in range(tile // CHUNK):
        vn = vh[n * CHUNK:(n + 1) * CHUNK]
        stack = jnp.concatenate(
            [jnp.where(lane_g // HEAD_DIM == hd, vn, 0.0).astype(BF16) for hd in range(N_HEADS)], axis=0)
        s_chunks.append(_dot(wcat_ref[...], stack) + sbias_ref[...])
    y_gmlp = u * jnp.concatenate(s_chunks, axis=0)
    ffn_chunks(7, 8)

    y = jnp.concatenate(
        [_rms(yk, g_grp_ref[:, k * GROUP:(k + 1) * GROUP]).astype(BF16)
         for k, yk in enumerate((y_conv, y_pool, y_four, y_gmlp))], axis=1)
    ffn_chunks(8, 9)

    def out_half(rows):
        m = _dot(y[rows, :], w_out_ref[...])
        x1 = x_ref[rows, :] + _rms(m, g_post_ref[...])
        return x1, _rms(x1, g_pre_ffn_ref[...]).astype(BF16)

    top, bot = slice(0, tile // 2), slice(tile // 2, tile)
    x1_top, h2_top = out_half(top)
    ffn_chunks(9, 10)
    x1_bot, h2_bot = out_half(bot)
    ffn_chunks(10, D_FF // FF_CHUNK)
    ffn_down(top)
    x1_s[top, :] = x1_top
    h2_s[top, :] = h2_top
    ffn_down(bot)
    x1_s[bot, :] = x1_bot
    h2_s[bot, :] = h2_bot


def _mixer_ffn(x2, spec, seq, layer, p):
    n_tok = x2.shape[0]
    tile = min(TOKEN_TILE, seq)
    block = MIXER_SUB * tile
    n_blocks = n_tok // block
    n_halo_blocks = n_tok // HALO
    per_block = block // HALO
    seq_hi = seq // SEQ_LO
    k_lo_blocks = seq_hi // V7X_SUBLANES
    tile_rows = tile // seq_hi * V7X_SUBLANES
    blocks_per_seq = seq // block
    spec = spec.reshape(spec.shape[:3] + (blocks_per_seq, MIXER_SUB, tile_rows, V7X_LANES))
    cur = lambda i: jnp.minimum(i, n_blocks - 1)
    return pl.pallas_call(
        functools.partial(_mixer_ffn_kernel, tile=tile, seq=seq, n_blocks=n_blocks),
        grid=(n_blocks + 1,),
        in_specs=[
            pl.BlockSpec((block, D_MODEL), lambda i: (cur(i), 0)),
            pl.BlockSpec((HALO, D_MODEL), lambda i: (jnp.maximum(cur(i) * per_block - 1, 0), 0)),
            pl.BlockSpec((HALO, D_MODEL),
                         lambda i: (jnp.minimum((cur(i) + 1) * per_block, n_halo_blocks - 1), 0)),
            pl.BlockSpec((GROUP // V7X_LANES, 1, k_lo_blocks, 1, MIXER_SUB, tile_rows, V7X_LANES),
                         lambda i: (0, cur(i) // blocks_per_seq, 0, cur(i) % blocks_per_seq, 0, 0, 0)),
            _layer_spec(layer, p["gains"].shape[1:]),
            _layer_spec(layer, p["vec"].shape[1:]),
            _layer_spec(layer, p["mats"].shape[1:]),
            _layer_spec(layer, p["wcat"].shape[1:]),
            _layer_spec(layer, p["w_rows"].shape[1:]),
            _layer_spec(layer, p["wd"].shape[1:]),
        ],
        out_specs=pl.BlockSpec((block, D_MODEL), lambda i: (jnp.maximum(i - 1, 0), 0)),
        out_shape=jax.ShapeDtypeStruct(x2.shape, x2.dtype),
        scratch_shapes=[
            pltpu.VMEM((tile + 2 * HALO, GROUP), F32),
            pltpu.VMEM((tile + 2 * HALO + V7X_SUBLANES, GROUP), F32),
            pltpu.VMEM((2, tile + 2 * HALO, V7X_LANES), F32),
            pltpu.VMEM((tile, D_FF), BF16),
            pltpu.VMEM((MIXER_SUB, tile, D_MODEL), F32),
            pltpu.VMEM((MIXER_SUB, tile, D_MODEL), BF16),
        ],
        compiler_params=pltpu.CompilerParams(
            dimension_semantics=("arbitrary",),
            vmem_limit_bytes=V7X_VMEM_LIMIT_BYTES),
        name="mixer_ffn",
    )(x2, x2, x2, spec, p["gains"], p["vec"], p["mats"], p["wcat"], p["w_rows"], p["wd"])


def _block_diag(blocks):
    depth, n, k, _ = blocks.shape
    on_diagonal = jnp.eye(n, dtype=bool)[None, :, None, :, None]
    return jnp.where(on_diagonal, blocks[:, :, :, None, :], 0.0).reshape(depth, n * k, n * k)


def _pack_params(pre_mix_gain, post_mix_gain, pre_ffn_gain, post_ffn_gain, w_in, conv_w, pool_w, pool_scale,
                 fourier_w, spatial_w, spatial_b, group_norm_gain, w_out, w_gate, w_up, w_down):
    depth = w_in.shape[0]
    sl = lambda k: w_in[:, :, k * GROUP:(k + 1) * GROUP]
    w_rows = jnp.concatenate([sl(1), sl(2), sl(3), sl(0), sl(5), sl(6), sl(4), w_out, w_gate, w_up],
                             axis=2).astype(BF16)
    assert w_rows.shape[2] == W_ROWS_COLS
    mavg = np.kron(np.eye(N_HEADS), np.full((HEAD_DIM, HEAD_DIM), 1.0 / HEAD_DIM))
    mavg = jnp.broadcast_to(jnp.asarray(mavg, F32), (depth, GROUP, GROUP))
    sbias = jnp.repeat(jnp.transpose(spatial_b, (0, 2, 1)), HEAD_DIM, axis=2)
    row = lambda g: g[:, None, :]
    return dict(
        gains=jnp.concatenate([row(pre_mix_gain), row(post_mix_gain), row(pre_ffn_gain), row(post_ffn_gain),
                               row(group_norm_gain), jnp.zeros((depth, V7X_SUBLANES - 5, D_MODEL), F32)],
                              axis=1),
        vec=jnp.concatenate([sbias, conv_w, row(pool_scale),
                             jnp.zeros((depth, V7X_SUBLANES - 4, GROUP), F32)], axis=1),
        mats=jnp.stack([_block_diag(pool_w), _block_diag(fourier_w), mavg], axis=1).astype(BF16),
        wcat=jnp.transpose(spatial_w, (0, 2, 1, 3)).reshape(depth, CHUNK, N_HEADS * CHUNK).astype(BF16),
        w_rows=w_rows,
        wd=w_down.astype(BF16),
    )


def kernel(x, pre_mix_gain, post_mix_gain, pre_ffn_gain, post_ffn_gain, w_in, conv_w, pool_w, pool_scale,
           fourier_w, spatial_w, spatial_b, group_norm_gain, w_out, w_gate, w_up, w_down):
    b, seq, d = x.shape
    assert d == D_MODEL and seq % SEQ_LO == 0 and seq % (MIXER_SUB * min(TOKEN_TILE, seq)) == 0
    tables = _fourier_tables(seq)
    p = _pack_params(pre_mix_gain, post_mix_gain, pre_ffn_gain, post_ffn_gain, w_in, conv_w, pool_w,
                     pool_scale, fourier_w, spatial_w, spatial_b, group_norm_gain, w_out, w_gate, w_up,
                     w_down)
    for layer in range(w_in.shape[0]):
        spec = _fourier_spec(x, layer, p, tables)
        x = _mixer_ffn(x.reshape(b * seq, d), spec, seq, layer, p).reshape(b, seq, d)
    return x
```

```python
import functools

import jax
import jax.numpy as jnp
import numpy as np
from jax import lax
from jax.experimental import pallas as pl
from jax.experimental.pallas import tpu as pltpu

D_MODEL = 1024
GROUP = 256
HEAD_DIM = 64
N_HEADS = GROUP // HEAD_DIM
CHUNK = 128
D_FF = 2816
POOL_WINDOWS = (2, 4, 8, 16)
EPS = 1e-6

V7X_LANES = 128
V7X_SUBLANES = 8
V7X_MXU_COLS = 256
V7X_VMEM_LIMIT_BYTES = 56 * 1024 * 1024

SEQ_LO = 128
HALO = 16
TOKEN_TILE = 512
MIXER_SUB = 1
FF_CHUNK = V7X_MXU_COLS
W_IN_COL = 0
W_F_COL = W_IN_COL + 6 * GROUP
W_OUT_COL = W_F_COL + GROUP
W_GATE_COL = W_OUT_COL + D_MODEL
W_UP_COL = W_GATE_COL + D_FF
W_ROWS_COLS = W_UP_COL + D_FF
FOURIER_RING = 3
FOURIER_SUB = 2

F32 = jnp.float32
BF16 = jnp.bfloat16


def _dot(a, b):
    return jnp.dot(a, b, preferred_element_type=F32)


def _rms(x, g):
    ms = jnp.mean(x * x, axis=-1, keepdims=True)
    return x * lax.rsqrt(ms + EPS) * g


def _fourier_tables(seq):
    seq_hi = seq // SEQ_LO
    cm = np.outer(np.arange(HEAD_DIM), np.arange(HEAD_DIM)) % HEAD_DIM
    ang = 2.0 * np.pi * cm / HEAD_DIM
    cos_bd = np.kron(np.eye(N_HEADS), np.cos(ang)) / np.sqrt(HEAD_DIM)
    sin_bd = np.kron(np.eye(N_HEADS), np.sin(ang)) / np.sqrt(HEAD_DIM)
    fc = np.block([[cos_bd, -sin_bd], [sin_bd, cos_bd]])
    kt = np.outer(np.arange(seq_hi), np.arange(seq_hi)) % seq_hi
    ang1 = 2.0 * np.pi * kt / seq_hi
    eye = np.eye(V7X_SUBLANES)
    c1 = np.kron(np.cos(ang1) / np.sqrt(seq_hi), eye)
    s1 = np.kron(np.sin(ang1) / np.sqrt(seq_hi), eye)
    kd = np.concatenate([c1, -s1], axis=0)
    k = np.arange(SEQ_LO)[None, :, None] * seq_hi + np.arange(seq_hi)[:, None, None]
    ang2 = 2.0 * np.pi * ((k * np.arange(SEQ_LO)[None, None, :]) % seq) / seq
    e = np.concatenate([np.cos(ang2), np.sin(ang2)], axis=2) / np.sqrt(SEQ_LO)
    return (jnp.asarray(fc, F32).astype(BF16), jnp.asarray(kd, F32).astype(BF16),
            jnp.asarray(e, F32).astype(BF16))


def _fourier_kernel(x_hbm, g_ref, wf_ref, fc_ref, kd_ref, e_ref, o_ref, y_s, f_s, x_ring, x_sem,
                    *, seq_hi, n_tb, n_blocks):
    step = pl.program_id(0)
    tb = jnp.maximum(step - 1, 0) % n_tb
    rows = seq_hi * V7X_SUBLANES
    t_lo_per_step = V7X_SUBLANES * FOURIER_SUB

    def x_copy(block):
        t_lo0 = pl.multiple_of((block % n_tb) * t_lo_per_step, t_lo_per_step)
        slot = block % FOURIER_RING
        return pltpu.make_async_copy(x_hbm.at[block // n_tb, :, pl.ds(t_lo0, t_lo_per_step), :],
                                     x_ring.at[slot], x_sem.at[slot])

    @pl.when(step == 0)
    def _():
        f_s[...] = jnp.zeros_like(f_s)
        x_copy(step).start()
        x_copy(step + 1).start()

    @pl.when(step + FOURIER_RING - 1 < n_blocks)
    def _():
        x_copy(step + FOURIER_RING - 1).start()

    @pl.when(step < n_blocks)
    def _():
        x_copy(step).wait()

    x_ref = x_ring.at[jnp.minimum(step, n_blocks - 1) % FOURIER_RING]

    def stage_norm(q):
        x = x_ref[:, q * V7X_SUBLANES:(q + 1) * V7X_SUBLANES, :].reshape(rows, D_MODEL)
        inv = lax.rsqrt(jnp.mean(x * x, axis=-1, keepdims=True) + EPS)
        return (x * g_ref[0:1, :]).astype(BF16), inv

    def stage_proj(q, h_inv):
        h, inv = h_inv
        return (_dot(h, wf_ref[...]) * inv).astype(BF16)

    def stage_slow_dft(q, f):
        u = _dot(kd_ref[...], f)
        return jnp.concatenate([u[:rows], u[rows:]], axis=1).astype(BF16)

    def stage_channel_dft(q, uc):
        return _dot(uc, fc_ref[...])

    vals = []
    for q in range(FOURIER_SUB):
        h_inv = stage_norm(q)
        uc = stage_slow_dft(q, f_s[q])
        f_s[q] = stage_proj(q, h_inv)
        vals.append(stage_channel_dft(q, uc))

    row = pl.multiple_of(tb * t_lo_per_step, t_lo_per_step)
    for k in range(seq_hi):
        rs = slice(k * V7X_SUBLANES, (k + 1) * V7X_SUBLANES)
        ls = slice(k * GROUP, (k + 1) * GROUP)
        piece = jnp.concatenate([v[rs] for v in vals], axis=0).astype(BF16)
        y_s[pl.ds(row, t_lo_per_step), ls] = piece[:, :GROUP]
        y_s[pl.ds(SEQ_LO + row, t_lo_per_step), ls] = piece[:, GROUP:]

    @pl.when((step > 0) & (tb == n_tb - 1))
    def _():
        def out_block(kb, carry):
            for j in range(V7X_SUBLANES):
                k_lo = kb * V7X_SUBLANES + j
                res = _dot(e_ref[k_lo], y_s[:, pl.ds(pl.multiple_of(k_lo * GROUP, GROUP), GROUP)])
                for half in range(GROUP // V7X_LANES):
                    ls = slice(half * V7X_LANES, (half + 1) * V7X_LANES)
                    o_ref[half, 0, kb, pl.ds(j, SEQ_LO, stride=V7X_SUBLANES), :] = res[:, ls]
            return carry

        lax.fori_loop(0, seq_hi // V7X_SUBLANES, out_block, 0)


def _const_spec(shape):
    n = len(shape)
    return pl.BlockSpec(shape, lambda *_: (0,) * n, pipeline_mode=pl.Buffered(1))


def _layer_spec(layer, shape, block_index=None):
    index = (0,) * len(shape) if block_index is None else block_index
    return pl.BlockSpec((None,) + shape, lambda *_: (layer,) + index, pipeline_mode=pl.Buffered(1))


def _fourier_spec(x, layer, params, tables):
    b, seq, _ = x.shape
    seq_hi = seq // SEQ_LO
    t_lo_per_step = V7X_SUBLANES * FOURIER_SUB
    n_tb = SEQ_LO // t_lo_per_step
    rows = seq_hi * V7X_SUBLANES
    fc, kd, e = tables
    halves = GROUP // V7X_LANES
    k_lo_blocks = seq_hi // V7X_SUBLANES
    out_block = (halves, 1, k_lo_blocks, SEQ_LO * V7X_SUBLANES, V7X_LANES)
    x4 = x.reshape(b, seq_hi, SEQ_LO, D_MODEL)
    n_blocks = b * n_tb
    assert n_blocks >= FOURIER_RING - 1
    dft_batch = lambda s: jnp.maximum(s - 1, 0) // n_tb
    out = pl.pallas_call(
        functools.partial(_fourier_kernel, seq_hi=seq_hi, n_tb=n_tb, n_blocks=n_blocks),
        grid=(n_blocks + 1,),
        in_specs=[
            pl.BlockSpec(memory_space=pl.ANY),
            _layer_spec(layer, (V7X_SUBLANES, D_MODEL)),
            _layer_spec(layer, (D_MODEL, GROUP), (0, W_F_COL // GROUP)),
            _const_spec((2 * GROUP, 2 * GROUP)),
            _const_spec((2 * rows, rows)),
            _const_spec((seq_hi, SEQ_LO, 2 * SEQ_LO)),
        ],
        out_specs=pl.BlockSpec(out_block, lambda s: (0, dft_batch(s), 0, 0, 0)),
        out_shape=jax.ShapeDtypeStruct((halves, b) + out_block[2:], F32),
        scratch_shapes=[
            pltpu.VMEM((2 * SEQ_LO, seq_hi * GROUP), BF16),
            pltpu.VMEM((FOURIER_SUB, rows, GROUP), BF16),
            pltpu.VMEM((FOURIER_RING, seq_hi, t_lo_per_step, D_MODEL), F32),
            pltpu.SemaphoreType.DMA((FOURIER_RING,)),
        ],
        compiler_params=pltpu.CompilerParams(
            dimension_semantics=("arbitrary",),
            vmem_limit_bytes=V7X_VMEM_LIMIT_BYTES),
        name="fourier_spec",
    )(x4, params["gains"], params["w_rows"], fc, kd, e)
    return out


def _mixer_ffn_kernel(x_ref, xp_ref, xn_ref, spec_ref, gains_ref, vec_ref, mats_ref, wcat_ref, w_rows_ref,
                      wd_ref,
                      o_ref, cz_s, p_s, ps_s, act_s, x1_all_s, h2_all_s, *, tile, seq, n_blocks):
    g_pre_ref, g_post_ref, g_pre_ffn_ref, g_post_ffn_ref, g_grp_ref = (
        gains_ref.at[k:k + 1] for k in range(5))
    sbias_ref = vec_ref.at[0:CHUNK]
    conv_w_ref = vec_ref.at[CHUNK:CHUNK + 3]
    pool_scale_ref = vec_ref.at[CHUNK + 3:CHUNK + 4]
    pool_w_ref, four_w_ref, mavg_ref = (mats_ref.at[k] for k in range(3))
    w_in_ref = w_rows_ref.at[:, W_IN_COL:W_IN_COL + 6 * GROUP]
    w_out_ref = w_rows_ref.at[:, W_OUT_COL:W_OUT_COL + D_MODEL]
    wg_ref = w_rows_ref.at[:, W_GATE_COL:W_GATE_COL + D_FF]
    wu_ref = w_rows_ref.at[:, W_UP_COL:W_UP_COL + D_FF]
    step = pl.program_id(0)

    @pl.when(step == 0)
    def _():
        x1_all_s[...] = jnp.zeros_like(x1_all_s)
        h2_all_s[...] = jnp.zeros_like(h2_all_s)

    blk = jnp.minimum(step, n_blocks - 1)
    tiles_per_seq = seq // tile
    for k in range(MIXER_SUB):
        rows_k = pl.ds(k * tile, tile)
        prev_ref = xp_ref if k == 0 else x_ref.at[pl.ds(k * tile - HALO, HALO)]
        next_ref = xn_ref if k == MIXER_SUB - 1 else x_ref.at[pl.ds((k + 1) * tile, HALO)]
        _mixer_ffn_tile(
            x_ref.at[rows_k], prev_ref, next_ref, spec_ref.at[:, 0, :, 0, k], o_ref.at[rows_k],
            x1_all_s.at[k], h2_all_s.at[k], (blk * MIXER_SUB + k) % tiles_per_seq, tiles_per_seq,
            g_pre_ref, g_post_ref, g_pre_ffn_ref, g_post_ffn_ref, g_grp_ref, sbias_ref, conv_w_ref,
            pool_scale_ref, pool_w_ref, four_w_ref, mavg_ref, wcat_ref, w_in_ref, w_out_ref, wg_ref, wu_ref,
            wd_ref, cz_s, p_s, ps_s, act_s, tile=tile, seq=seq)


def _mixer_ffn_tile(x_ref, xp_ref, xn_ref, spec_ref, o_ref, x1_s, h2_s, ti, tiles_per_seq,
                    g_pre_ref, g_post_ref, g_pre_ffn_ref, g_post_ffn_ref, g_grp_ref, sbias_ref, conv_w_ref,
                    pool_scale_ref, pool_w_ref, four_w_ref, mavg_ref, wcat_ref, w_in_ref, w_out_ref, wg_ref,
                    wu_ref, wd_ref, cz_s, p_s, ps_s, act_s, *, tile, seq):
    def ffn_chunks(lo, hi):
        for j in range(lo, hi):
            ls = slice(j * FF_CHUNK, (j + 1) * FF_CHUNK)
            gate = _dot(h2_s[...], wg_ref[:, ls])
            up = _dot(h2_s[...], wu_ref[:, ls])
            act_s[:, ls] = (gate * jax.nn.sigmoid(gate) * up).astype(BF16)

    def ffn_down(rows):
        f = _dot(act_s[rows, :], wd_ref[...])
        o_ref[rows, :] = x1_s[rows, :] + _rms(f, g_post_ffn_ref[...])

    ffn_chunks(0, 1)

    x = x_ref[...]
    xh = jnp.concatenate([xp_ref[...], xn_ref[...]], axis=0)

    hm = _rms(x, g_pre_ref[...]).astype(BF16)
    hh = _rms(xh, g_pre_ref[...]).astype(BF16)
    ffn_chunks(1, 2)
    pe = _dot(jnp.concatenate([hm, hh], axis=0), w_in_ref[:, :3 * GROUP])
    pm = jnp.concatenate([pe[:tile], _dot(hm, w_in_ref[:, 3 * GROUP:])], axis=1)
    ph = pe[tile:]
    ffn_chunks(2, 3)
    hrow = lax.broadcasted_iota(jnp.int32, (2 * HALO, 1), 0)
    has_prev = (ti != 0).astype(jnp.int32)
    has_next = (ti != tiles_per_seq - 1).astype(jnp.int32)
    ph = jnp.where(jnp.where(hrow < HALO, has_prev, has_next) != 0, ph, 0.0)

    cz_s[pl.ds(0, HALO), :] = ph[:HALO, :GROUP] * ph[:HALO, GROUP:2 * GROUP]
    cz_s[pl.ds(HALO, tile), :] = pm[:, :GROUP] * pm[:, GROUP:2 * GROUP]
    cz_s[pl.ds(HALO + tile, HALO), :] = ph[HALO:, :GROUP] * ph[HALO:, GROUP:2 * GROUP]
    p_s[pl.ds(0, HALO), :] = ph[:HALO, 2 * GROUP:]
    p_s[pl.ds(HALO, tile), :] = pm[:, 2 * GROUP:3 * GROUP]
    p_s[pl.ds(HALO + tile, HALO), :] = ph[HALO:, 2 * GROUP:]
    p_s[pl.ds(2 * HALO + tile, V7X_SUBLANES), :] = jnp.zeros((V7X_SUBLANES, GROUP), F32)
    b_gate = pm[:, 3 * GROUP:4 * GROUP]
    u = pm[:, 4 * GROUP:5 * GROUP]
    v = pm[:, 5 * GROUP:]

    y_conv = b_gate * (conv_w_ref[0:1, :] * cz_s[pl.ds(HALO - 1, tile), :]
                       + conv_w_ref[1:2, :] * cz_s[pl.ds(HALO, tile), :]
                       + conv_w_ref[2:3, :] * cz_s[pl.ds(HALO + 1, tile), :])
    ffn_chunks(3, 4)

    pos = ti * tile + lax.broadcasted_iota(jnp.int32, (tile, 1), 0)
    lane = lax.broadcasted_iota(jnp.int32, (1, V7X_LANES), 1)
    first_half = lane < HEAD_DIM

    def pool_col(col, inner, outer):
        w_small, w_big = POOL_WINDOWS[2 * col], POOL_WINDOWS[2 * col + 1]
        half = jnp.where(first_half, w_small // 2, w_big // 2)
        cnt = (jnp.minimum(pos + half, seq) - jnp.maximum(pos - half, 0)).astype(F32)
        mean = jnp.where(first_half, inner, outer) / cnt
        return mean - p_s[pl.ds(HALO, tile), col * V7X_LANES:(col + 1) * V7X_LANES]

    ls0 = slice(0, V7X_LANES)
    inner0 = p_s[pl.ds(HALO - 1, tile), ls0] + p_s[pl.ds(HALO, tile), ls0]
    outer0 = inner0 + p_s[pl.ds(HALO - 2, tile), ls0] + p_s[pl.ds(HALO + 1, tile), ls0]
    d_col0 = pool_col(0, inner0, outer0)
    ffn_chunks(4, 5)
    ls1 = slice(V7X_LANES, 2 * V7X_LANES)
    base = HALO - POOL_WINDOWS[3] // 2
    n2, n4, n8 = tile + 24, tile + 16, tile + 8
    ps_s[0, pl.ds(base, n2), :] = p_s[pl.ds(base, n2), ls1] + p_s[pl.ds(base + 1, n2), ls1]
    ps_s[1, pl.ds(base, n4), :] = ps_s[0, pl.ds(base, n4), :] + ps_s[0, pl.ds(base + 2, n4), :]
    ps_s[0, pl.ds(base, n8), :] = ps_s[1, pl.ds(base, n8), :] + ps_s[1, pl.ds(base + 4, n8), :]
    inner1 = ps_s[0, pl.ds(HALO - 4, tile), :]
    outer1 = ps_s[0, pl.ds(HALO - 8, tile), :] + ps_s[0, pl.ds(HALO, tile), :]
    d = jnp.concatenate([d_col0, pool_col(1, inner1, outer1)], axis=1).astype(BF16)
    y_pool = _dot(d, pool_w_ref[...]) * pool_scale_ref[...]

    n_kb, n_kh = spec_ref.shape[1], spec_ref.shape[2] // V7X_SUBLANES
    spec = jnp.concatenate(
        [jnp.concatenate([spec_ref[half, kb, pl.ds(i * V7X_SUBLANES, V7X_SUBLANES), :]
                          for i in range(n_kh) for kb in range(n_kb)], axis=0)
         for half in range(GROUP // V7X_LANES)], axis=1).astype(BF16)
    y_four = _dot(spec, four_w_ref[...])
    ffn_chunks(5, 6)

    mu = _dot(v.astype(BF16), mavg_ref[...])
    vc = v - mu
    var = _dot((vc * vc).astype(BF16), mavg_ref[...])
    vh = vc * lax.rsqrt(var + EPS)
    ffn_chunks(6, 7)
    lane_g = lax.broadcasted_iota(jnp.int32, (1, GROUP), 1)
    s_chunks = []
    for n in range(tile // CHUNK):
        vn = vh[n * CHUNK:(n + 1) * CHUNK]
        stack = jnp.concatenate(
            [jnp.where(lane_g // HEAD_DIM == hd, vn, 0.0).astype(BF16) for hd in range(N_HEADS)], axis=0)
        s_chunks.append(_dot(wcat_ref[...], stack) + sbias_ref[...])
    y_gmlp = u * jnp.concatenate(s_chunks, axis=0)
    ffn_chunks(7, 8)

    y = jnp.concatenate(
        [_rms(yk, g_grp_ref[:, k * GROUP:(k + 1) * GROUP]).astype(BF16)
         for k, yk in enumerate((y_conv, y_pool, y_four, y_gmlp))], axis=1)
    ffn_chunks(8, 9)

    def out_half(rows):
        m = _dot(y[rows, :], w_out_ref[...])
        x1 = x_ref[rows, :] + _rms(m, g_post_ref[...])
        return x1, _rms(x1, g_pre_ffn_ref[...]).astype(BF16)

    top, bot = slice(0, tile // 2), slice(tile // 2, tile)
    x1_top, h2_top = out_half(top)
    ffn_chunks(9, 10)
    x1_bot, h2_bot = out_half(bot)
    ffn_chunks(10, D_FF // FF_CHUNK)
    ffn_down(top)
    x1_s[top, :] = x1_top
    h2_s[top, :] = h2_top
    ffn_down(bot)
    x1_s[bot, :] = x1_bot
    h2_s[bot, :] = h2_bot


def _mixer_ffn(x2, spec, seq, layer, p):
    n_tok = x2.shape[0]
    tile = min(TOKEN_TILE, seq)
    block = MIXER_SUB * tile
    n_blocks = n_tok // block
    n_halo_blocks = n_tok // HALO
    per_block = block // HALO
    seq_hi = seq // SEQ_LO
    k_lo_blocks = seq_hi // V7X_SUBLANES
    tile_rows = tile // seq_hi * V7X_SUBLANES
    blocks_per_seq = seq // block
    spec = spec.reshape(spec.shape[:3] + (blocks_per_seq, MIXER_SUB, tile_rows, V7X_LANES))
    cur = lambda i: jnp.minimum(i, n_blocks - 1)
    return pl.pallas_call(
        functools.partial(_mixer_ffn_kernel, tile=tile, seq=seq, n_blocks=n_blocks),
        grid=(n_blocks + 1,),
        in_specs=[
            pl.BlockSpec((block, D_MODEL), lambda i: (cur(i), 0)),
            pl.BlockSpec((HALO, D_MODEL), lambda i: (jnp.maximum(cur(i) * per_block - 1, 0), 0)),
            pl.BlockSpec((HALO, D_MODEL),
                         lambda i: (jnp.minimum((cur(i) + 1) * per_block, n_halo_blocks - 1), 0)),
            pl.BlockSpec((GROUP // V7X_LANES, 1, k_lo_blocks, 1, MIXER_SUB, tile_rows, V7X_LANES),
                         lambda i: (0, cur(i) // blocks_per_seq, 0, cur(i) % blocks_per_seq, 0, 0, 0)),
            _layer_spec(layer, p["gains"].shape[1:]),
            _layer_spec(layer, p["vec"].shape[1:]),
            _layer_spec(layer, p["mats"].shape[1:]),
            _layer_spec(layer, p["wcat"].shape[1:]),
            _layer_spec(layer, p["w_rows"].shape[1:]),
            _layer_spec(layer, p["wd"].shape[1:]),
        ],
        out_specs=pl.BlockSpec((block, D_MODEL), lambda i: (jnp.maximum(i - 1, 0), 0)),
        out_shape=jax.ShapeDtypeStruct(x2.shape, x2.dtype),
        scratch_shapes=[
            pltpu.VMEM((tile + 2 * HALO, GROUP), F32),
            pltpu.VMEM((tile + 2 * HALO + V7X_SUBLANES, GROUP), F32),
            pltpu.VMEM((2, tile + 2 * HALO, V7X_LANES), F32),
            pltpu.VMEM((tile, D_FF), BF16),
            pltpu.VMEM((MIXER_SUB, tile, D_MODEL), F32),
            pltpu.VMEM((MIXER_SUB, tile, D_MODEL), BF16),
        ],
        compiler_params=pltpu.CompilerParams(
            dimension_semantics=("arbitrary",),
            vmem_limit_bytes=V7X_VMEM_LIMIT_BYTES),
        name="mixer_ffn",
    )(x2, x2, x2, spec, p["gains"], p["vec"], p["mats"], p["wcat"], p["w_rows"], p["wd"])


def _block_diag(blocks):
    depth, n, k, _ = blocks.shape
    on_diagonal = jnp.eye(n, dtype=bool)[None, :, None, :, None]
    return jnp.where(on_diagonal, blocks[:, :, :, None, :], 0.0).reshape(depth, n * k, n * k)


def _pack_params(pre_mix_gain, post_mix_gain, pre_ffn_gain, post_ffn_gain, w_in, conv_w, pool_w, pool_scale,
                 fourier_w, spatial_w, spatial_b, group_norm_gain, w_out, w_gate, w_up, w_down):
    depth = w_in.shape[0]
    sl = lambda k: w_in[:, :, k * GROUP:(k + 1) * GROUP]
    w_rows = jnp.concatenate([sl(1), sl(2), sl(3), sl(0), sl(5), sl(6), sl(4), w_out, w_gate, w_up],
                             axis=2).astype(BF16)
    assert w_rows.shape[2] == W_ROWS_COLS
    mavg = np.kron(np.eye(N_HEADS), np.full((HEAD_DIM, HEAD_DIM), 1.0 / HEAD_DIM))
    mavg = jnp.broadcast_to(jnp.asarray(mavg, F32), (depth, GROUP, GROUP))
    sbias = jnp.repeat(jnp.transpose(spatial_b, (0, 2, 1)), HEAD_DIM, axis=2)
    row = lambda g: g[:, None, :]
    return dict(
        gains=jnp.concatenate([row(pre_mix_gain), row(post_mix_gain), row(pre_ffn_gain), row(post_ffn_gain),
                               row(group_norm_gain), jnp.zeros((depth, V7X_SUBLANES - 5, D_MODEL), F32)],
                              axis=1),
        vec=jnp.concatenate([sbias, conv_w, row(pool_scale),
                             jnp.zeros((depth, V7X_SUBLANES - 4, GROUP), F32)], axis=1),
        mats=jnp.stack([_block_diag(pool_w), _block_diag(fourier_w), mavg], axis=1).astype(BF16),
        wcat=jnp.transpose(spatial_w, (0, 2, 1, 3)).reshape(depth, CHUNK, N_HEADS * CHUNK).astype(BF16),
        w_rows=w_rows,
        wd=w_down.astype(BF16),
    )


def kernel(x, pre_mix_gain, post_mix_gain, pre_ffn_gain, post_ffn_gain, w_in, conv_w, pool_w, pool_scale,
           fourier_w, spatial_w, spatial_b, group_norm_gain, w_out, w_gate, w_up, w_down):
    b, seq, d = x.shape
    assert d == D_MODEL and seq % SEQ_LO == 0 and seq % (MIXER_SUB * min(TOKEN_TILE, seq)) == 0
    tables = _fourier_tables(seq)
    p = _pack_params(pre_mix_gain, post_mix_gain, pre_ffn_gain, post_ffn_gain, w_in, conv_w, pool_w,
                     pool_scale, fourier_w, spatial_w, spatial_b, group_norm_gain, w_out, w_gate, w_up,
                     w_down)
    for layer in range(w_in.shape[0]):
        spec = _fourier_spec(x, layer, p, tables)
        x = _mixer_ffn(x.reshape(b * seq, d), spec, seq, layer, p).reshape(b, seq, d)
    return x
```

```python
import functools

import jax
import jax.numpy as jnp
import numpy as np
from jax import lax
from jax.experimental import pallas as pl
from jax.experimental.pallas import tpu as pltpu

D_MODEL = 1024
GROUP = 256
HEAD_DIM = 64
N_HEADS = GROUP // HEAD_DIM
CHUNK = 128
D_FF = 2816
POOL_WINDOWS = (2, 4, 8, 16)
EPS = 1e-6

V7X_LANES = 128
V7X_SUBLANES = 8
V7X_MXU_COLS = 256
V7X_VMEM_LIMIT_BYTES = 56 * 1024 * 1024

SEQ_LO = 128
HALO = 16
TOKEN_TILE = 512
MIXER_SUB = 1
FF_CHUNK = V7X_MXU_COLS
W_IN_COL = 0
W_F_COL = W_IN_COL + 6 * GROUP
W_OUT_COL = W_F_COL + GROUP
W_GATE_COL = W_OUT_COL + D_MODEL
W_UP_COL = W_GATE_COL + D_FF
W_ROWS_COLS = W_UP_COL + D_FF
FOURIER_SUB = 4

F32 = jnp.float32
BF16 = jnp.bfloat16


def _dot(a, b):
    return jnp.dot(a, b, preferred_element_type=F32)


def _rms(x, g):
    ms = jnp.mean(x * x, axis=-1, keepdims=True)
    return x * lax.rsqrt(ms + EPS) * g


def _fourier_tables(seq):
    seq_hi = seq // SEQ_LO
    cm = np.outer(np.arange(HEAD_DIM), np.arange(HEAD_DIM)) % HEAD_DIM
    ang = 2.0 * np.pi * cm / HEAD_DIM
    cos_bd = np.kron(np.eye(N_HEADS), np.cos(ang)) / np.sqrt(HEAD_DIM)
    sin_bd = np.kron(np.eye(N_HEADS), np.sin(ang)) / np.sqrt(HEAD_DIM)
    fc = np.block([[cos_bd, -sin_bd], [sin_bd, cos_bd]])
    kt = np.outer(np.arange(seq_hi), np.arange(seq_hi)) % seq_hi
    ang1 = 2.0 * np.pi * kt / seq_hi
    eye = np.eye(V7X_SUBLANES)
    c1 = np.kron(np.cos(ang1) / np.sqrt(seq_hi), eye)
    s1 = np.kron(np.sin(ang1) / np.sqrt(seq_hi), eye)
    kd = np.concatenate([c1, -s1], axis=0)
    k = np.arange(SEQ_LO)[None, :, None] * seq_hi + np.arange(seq_hi)[:, None, None]
    ang2 = 2.0 * np.pi * ((k * np.arange(SEQ_LO)[None, None, :]) % seq) / seq
    e = np.concatenate([np.cos(ang2), np.sin(ang2)], axis=2) / np.sqrt(SEQ_LO)
    return (jnp.asarray(fc, F32).astype(BF16), jnp.asarray(kd, F32).astype(BF16),
            jnp.asarray(e, F32).astype(BF16))


def _fourier_kernel(x_ref, g_ref, wf_ref, fc_ref, kd_ref, e_ref, o_ref, y_s, f_s, *, seq_hi, n_tb):
    step = pl.program_id(0)
    tb = jnp.maximum(step - 1, 0) % n_tb
    rows = seq_hi * V7X_SUBLANES
    t_lo_per_step = V7X_SUBLANES * FOURIER_SUB

    @pl.when(step == 0)
    def _():
        f_s[...] = jnp.zeros_like(f_s)

    def stage_norm(q):
        x = x_ref[0, :, q * V7X_SUBLANES:(q + 1) * V7X_SUBLANES, :].reshape(rows, D_MODEL)
        inv = lax.rsqrt(jnp.mean(x * x, axis=-1, keepdims=True) + EPS)
        return (x * g_ref[0:1, :]).astype(BF16), inv

    def stage_proj(q, h_inv):
        h, inv = h_inv
        return (_dot(h, wf_ref[...]) * inv).astype(BF16)

    def stage_slow_dft(q, f):
        u = _dot(kd_ref[...], f)
        return jnp.concatenate([u[:rows], u[rows:]], axis=1).astype(BF16)

    def stage_channel_dft(q, uc):
        return _dot(uc, fc_ref[...])

    vals = []
    for q in range(FOURIER_SUB):
        h_inv = stage_norm(q)
        uc = stage_slow_dft(q, f_s[q])
        f_s[q] = stage_proj(q, h_inv)
        vals.append(stage_channel_dft(q, uc))

    row = pl.multiple_of(tb * t_lo_per_step, t_lo_per_step)
    for k in range(seq_hi):
        rs = slice(k * V7X_SUBLANES, (k + 1) * V7X_SUBLANES)
        ls = slice(k * GROUP, (k + 1) * GROUP)
        piece = jnp.concatenate([v[rs] for v in vals], axis=0).astype(BF16)
        y_s[pl.ds(row, t_lo_per_step), ls] = piece[:, :GROUP]
        y_s[pl.ds(SEQ_LO + row, t_lo_per_step), ls] = piece[:, GROUP:]

    @pl.when((step > 0) & (tb == n_tb - 1))
    def _():
        def out_block(kb, carry):
            for j in range(V7X_SUBLANES):
                k_lo = kb * V7X_SUBLANES + j
                res = _dot(e_ref[k_lo], y_s[:, pl.ds(pl.multiple_of(k_lo * GROUP, GROUP), GROUP)])
                for half in range(GROUP // V7X_LANES):
                    ls = slice(half * V7X_LANES, (half + 1) * V7X_LANES)
                    o_ref[half, 0, kb, pl.ds(j, SEQ_LO, stride=V7X_SUBLANES), :] = res[:, ls]
            return carry

        lax.fori_loop(0, seq_hi // V7X_SUBLANES, out_block, 0)


def _const_spec(shape):
    n = len(shape)
    return pl.BlockSpec(shape, lambda *_: (0,) * n, pipeline_mode=pl.Buffered(1))


def _layer_spec(layer, shape, block_index=None):
    index = (0,) * len(shape) if block_index is None else block_index
    return pl.BlockSpec((None,) + shape, lambda *_: (layer,) + index, pipeline_mode=pl.Buffered(1))


def _fourier_spec(x, layer, params, tables):
    b, seq, _ = x.shape
    seq_hi = seq // SEQ_LO
    t_lo_per_step = V7X_SUBLANES * FOURIER_SUB
    n_tb = SEQ_LO // t_lo_per_step
    rows = seq_hi * V7X_SUBLANES
    fc, kd, e = tables
    halves = GROUP // V7X_LANES
    k_lo_blocks = seq_hi // V7X_SUBLANES
    out_block = (halves, 1, k_lo_blocks, SEQ_LO * V7X_SUBLANES, V7X_LANES)
    x4 = x.reshape(b, seq_hi, SEQ_LO, D_MODEL)
    n_blocks = b * n_tb
    proj_block = lambda s: jnp.minimum(s, n_blocks - 1)
    dft_batch = lambda s: jnp.maximum(s - 1, 0) // n_tb
    out = pl.pallas_call(
        functools.partial(_fourier_kernel, seq_hi=seq_hi, n_tb=n_tb),
        grid=(n_blocks + 1,),
        in_specs=[
            pl.BlockSpec((1, seq_hi, t_lo_per_step, D_MODEL),
                         lambda s: (proj_block(s) // n_tb, 0, proj_block(s) % n_tb, 0)),
            _layer_spec(layer, (V7X_SUBLANES, D_MODEL)),
            _layer_spec(layer, (D_MODEL, GROUP), (0, W_F_COL // GROUP)),
            _const_spec((2 * GROUP, 2 * GROUP)),
            _const_spec((2 * rows, rows)),
            _const_spec((seq_hi, SEQ_LO, 2 * SEQ_LO)),
        ],
        out_specs=pl.BlockSpec(out_block, lambda s: (0, dft_batch(s), 0, 0, 0)),
        out_shape=jax.ShapeDtypeStruct((halves, b) + out_block[2:], F32),
        scratch_shapes=[
            pltpu.VMEM((2 * SEQ_LO, seq_hi * GROUP), BF16),
            pltpu.VMEM((FOURIER_SUB, rows, GROUP), BF16),
        ],
        compiler_params=pltpu.CompilerParams(
            dimension_semantics=("arbitrary",),
            vmem_limit_bytes=V7X_VMEM_LIMIT_BYTES),
        name="fourier_spec",
    )(x4, params["gains"], params["w_rows"], fc, kd, e)
    return out


def _mixer_ffn_kernel(x_ref, xp_ref, xn_ref, spec_ref, gains_ref, vec_ref, mats_ref, wcat_ref, w_rows_ref,
                      wd_ref,
                      o_ref, cz_s, p_s, ps_s, act_s, x1_all_s, h2_all_s, wv_s, *, tile, seq, n_blocks):
    g_pre_ref, g_post_ref, g_pre_ffn_ref, g_post_ffn_ref, g_grp_ref = (
        gains_ref.at[k:k + 1] for k in range(5))
    sbias_ref = vec_ref.at[0:CHUNK]
    conv_w_ref = vec_ref.at[CHUNK:CHUNK + 3]
    pool_scale_ref = vec_ref.at[CHUNK + 3:CHUNK + 4]
    pool_w_ref, four_w_ref, mavg_ref = (mats_ref.at[k] for k in range(3))
    w_in_ref = w_rows_ref.at[:, W_IN_COL:W_IN_COL + 6 * GROUP]
    w_out_ref = w_rows_ref.at[:, W_OUT_COL:W_OUT_COL + D_MODEL]
    wg_ref = w_rows_ref.at[:, W_GATE_COL:W_GATE_COL + D_FF]
    wu_ref = w_rows_ref.at[:, W_UP_COL:W_UP_COL + D_FF]
    step = pl.program_id(0)

    @pl.when(step == 0)
    def _():
        x1_all_s[...] = jnp.zeros_like(x1_all_s)
        h2_all_s[...] = jnp.zeros_like(h2_all_s)
        wv = w_in_ref[:, 5 * GROUP:6 * GROUP]
        wv_s[...] = (wv.astype(F32) - _dot(wv, mavg_ref[...])).astype(BF16)

    blk = jnp.minimum(step, n_blocks - 1)
    tiles_per_seq = seq // tile
    for k in range(MIXER_SUB):
        rows_k = pl.ds(k * tile, tile)
        prev_ref = xp_ref if k == 0 else x_ref.at[pl.ds(k * tile - HALO, HALO)]
        next_ref = xn_ref if k == MIXER_SUB - 1 else x_ref.at[pl.ds((k + 1) * tile, HALO)]
        _mixer_ffn_tile(
            x_ref.at[rows_k], prev_ref, next_ref, spec_ref.at[:, 0, :, 0, k], o_ref.at[rows_k],
            x1_all_s.at[k], h2_all_s.at[k], (blk * MIXER_SUB + k) % tiles_per_seq, tiles_per_seq,
            g_pre_ref, g_post_ref, g_pre_ffn_ref, g_post_ffn_ref, g_grp_ref, sbias_ref, conv_w_ref,
            pool_scale_ref, pool_w_ref, four_w_ref, mavg_ref, wcat_ref, w_in_ref, w_out_ref, wg_ref, wu_ref,
            wd_ref, wv_s, cz_s, p_s, ps_s, act_s, tile=tile, seq=seq)


def _mixer_ffn_tile(x_ref, xp_ref, xn_ref, spec_ref, o_ref, x1_s, h2_s, ti, tiles_per_seq,
                    g_pre_ref, g_post_ref, g_pre_ffn_ref, g_post_ffn_ref, g_grp_ref, sbias_ref, conv_w_ref,
                    pool_scale_ref, pool_w_ref, four_w_ref, mavg_ref, wcat_ref, w_in_ref, w_out_ref, wg_ref,
                    wu_ref, wd_ref, wv_s, cz_s, p_s, ps_s, act_s, *, tile, seq):
    def ffn_chunks(lo, hi):
        for j in range(lo, hi):
            ls = slice(j * FF_CHUNK, (j + 1) * FF_CHUNK)
            gate = _dot(h2_s[...], wg_ref[:, ls])
            up = _dot(h2_s[...], wu_ref[:, ls])
            act_s[:, ls] = (gate * jax.nn.sigmoid(gate) * up).astype(BF16)

    def ffn_down(rows):
        f = _dot(act_s[rows, :], wd_ref[...])
        o_ref[rows, :] = x1_s[rows, :] + _rms(f, g_post_ffn_ref[...])

    ffn_chunks(0, 1)

    x = x_ref[...]
    xh = jnp.concatenate([xp_ref[...], xn_ref[...]], axis=0)

    hm = _rms(x, g_pre_ref[...]).astype(BF16)
    hh = _rms(xh, g_pre_ref[...]).astype(BF16)
    ffn_chunks(1, 2)
    pe = _dot(jnp.concatenate([hm, hh], axis=0), w_in_ref[:, :3 * GROUP])
    pm = jnp.concatenate([pe[:tile], _dot(hm, w_in_ref[:, 3 * GROUP:5 * GROUP])], axis=1)
    vc = _dot(hm, wv_s[...])
    ph = pe[tile:]
    ffn_chunks(2, 3)
    hrow = lax.broadcasted_iota(jnp.int32, (2 * HALO, 1), 0)
    has_prev = (ti != 0).astype(jnp.int32)
    has_next = (ti != tiles_per_seq - 1).astype(jnp.int32)
    ph = jnp.where(jnp.where(hrow < HALO, has_prev, has_next) != 0, ph, 0.0)

    cz_s[pl.ds(0, HALO), :] = ph[:HALO, :GROUP] * ph[:HALO, GROUP:2 * GROUP]
    cz_s[pl.ds(HALO, tile), :] = pm[:, :GROUP] * pm[:, GROUP:2 * GROUP]
    cz_s[pl.ds(HALO + tile, HALO), :] = ph[HALO:, :GROUP] * ph[HALO:, GROUP:2 * GROUP]
    p_s[pl.ds(0, HALO), :] = ph[:HALO, 2 * GROUP:]
    p_s[pl.ds(HALO, tile), :] = pm[:, 2 * GROUP:3 * GROUP]
    p_s[pl.ds(HALO + tile, HALO), :] = ph[HALO:, 2 * GROUP:]
    p_s[pl.ds(2 * HALO + tile, V7X_SUBLANES), :] = jnp.zeros((V7X_SUBLANES, GROUP), F32)
    b_gate = pm[:, 3 * GROUP:4 * GROUP]
    u = pm[:, 4 * GROUP:5 * GROUP]

    y_conv = b_gate * (conv_w_ref[0:1, :] * cz_s[pl.ds(HALO - 1, tile), :]
                       + conv_w_ref[1:2, :] * cz_s[pl.ds(HALO, tile), :]
                       + conv_w_ref[2:3, :] * cz_s[pl.ds(HALO + 1, tile), :])
    ffn_chunks(3, 4)

    pos = ti * tile + lax.broadcasted_iota(jnp.int32, (tile, 1), 0)
    lane = lax.broadcasted_iota(jnp.int32, (1, V7X_LANES), 1)
    first_half = lane < HEAD_DIM

    def pool_col(col, inner, outer):
        w_small, w_big = POOL_WINDOWS[2 * col], POOL_WINDOWS[2 * col + 1]
        half = jnp.where(first_half, w_small // 2, w_big // 2)
        cnt = (jnp.minimum(pos + half, seq) - jnp.maximum(pos - half, 0)).astype(F32)
        mean = jnp.where(first_half, inner, outer) / cnt
        return mean - p_s[pl.ds(HALO, tile), col * V7X_LANES:(col + 1) * V7X_LANES]

    ls0 = slice(0, V7X_LANES)
    inner0 = p_s[pl.ds(HALO - 1, tile), ls0] + p_s[pl.ds(HALO, tile), ls0]
    outer0 = inner0 + p_s[pl.ds(HALO - 2, tile), ls0] + p_s[pl.ds(HALO + 1, tile), ls0]
    d_col0 = pool_col(0, inner0, outer0)
    ffn_chunks(4, 5)
    ls1 = slice(V7X_LANES, 2 * V7X_LANES)
    base = HALO - POOL_WINDOWS[3] // 2
    n2, n4, n8 = tile + 24, tile + 16, tile + 8
    ps_s[0, pl.ds(base, n2), :] = p_s[pl.ds(base, n2), ls1] + p_s[pl.ds(base + 1, n2), ls1]
    ps_s[1, pl.ds(base, n4), :] = ps_s[0, pl.ds(base, n4), :] + ps_s[0, pl.ds(base + 2, n4), :]
    ps_s[0, pl.ds(base, n8), :] = ps_s[1, pl.ds(base, n8), :] + ps_s[1, pl.ds(base + 4, n8), :]
    inner1 = ps_s[0, pl.ds(HALO - 4, tile), :]
    outer1 = ps_s[0, pl.ds(HALO - 8, tile), :] + ps_s[0, pl.ds(HALO, tile), :]
    d = jnp.concatenate([d_col0, pool_col(1, inner1, outer1)], axis=1).astype(BF16)
    y_pool = _dot(d, pool_w_ref[...]) * pool_scale_ref[...]

    n_kb, n_kh = spec_ref.shape[1], spec_ref.shape[2] // V7X_SUBLANES
    spec = jnp.concatenate(
        [jnp.concatenate([spec_ref[half, kb, pl.ds(i * V7X_SUBLANES, V7X_SUBLANES), :]
                          for i in range(n_kh) for kb in range(n_kb)], axis=0)
         for half in range(GROUP // V7X_LANES)], axis=1).astype(BF16)
    y_four = _dot(spec, four_w_ref[...])
    ffn_chunks(5, 6)

    var = _dot((vc * vc).astype(BF16), mavg_ref[...])
    vh = vc * lax.rsqrt(var + EPS)
    ffn_chunks(6, 7)
    lane_g = lax.broadcasted_iota(jnp.int32, (1, GROUP), 1)
    s_chunks = []
    for n in range(tile // CHUNK):
        vn = vh[n * CHUNK:(n + 1) * CHUNK]
        stack = jnp.concatenate(
            [jnp.where(lane_g // HEAD_DIM == hd, vn, 0.0).astype(BF16) for hd in range(N_HEADS)], axis=0)
        s_chunks.append(_dot(wcat_ref[...], stack) + sbias_ref[...])
    y_gmlp = u * jnp.concatenate(s_chunks, axis=0)
    ffn_chunks(7, 8)

    y = jnp.concatenate(
        [_rms(yk, g_grp_ref[:, k * GROUP:(k + 1) * GROUP]).astype(BF16)
         for k, yk in enumerate((y_conv, y_pool, y_four, y_gmlp))], axis=1)
    ffn_chunks(8, 9)

    def out_half(rows):
        m = _dot(y[rows, :], w_out_ref[...])
        x1 = x_ref[rows, :] + _rms(m, g_post_ref[...])
        return x1, _rms(x1, g_pre_ffn_ref[...]).astype(BF16)

    top, bot = slice(0, tile // 2), slice(tile // 2, tile)
    x1_top, h2_top = out_half(top)
    ffn_chunks(9, 10)
    x1_bot, h2_bot = out_half(bot)
    ffn_chunks(10, D_FF // FF_CHUNK)
    ffn_down(top)
    x1_s[top, :] = x1_top
    h2_s[top, :] = h2_top
    ffn_down(bot)
    x1_s[bot, :] = x1_bot
    h2_s[bot, :] = h2_bot


def _mixer_ffn(x2, spec, seq, layer, p):
    n_tok = x2.shape[0]
    tile = min(TOKEN_TILE, seq)
    block = MIXER_SUB * tile
    n_blocks = n_tok // block
    n_halo_blocks = n_tok // HALO
    per_block = block // HALO
    seq_hi = seq // SEQ_LO
    k_lo_blocks = seq_hi // V7X_SUBLANES
    tile_rows = tile // seq_hi * V7X_SUBLANES
    blocks_per_seq = seq // block
    spec = spec.reshape(spec.shape[:3] + (blocks_per_seq, MIXER_SUB, tile_rows, V7X_LANES))
    cur = lambda i: jnp.minimum(i, n_blocks - 1)
    return pl.pallas_call(
        functools.partial(_mixer_ffn_kernel, tile=tile, seq=seq, n_blocks=n_blocks),
        grid=(n_blocks + 1,),
        in_specs=[
            pl.BlockSpec((block, D_MODEL), lambda i: (cur(i), 0)),
            pl.BlockSpec((HALO, D_MODEL), lambda i: (jnp.maximum(cur(i) * per_block - 1, 0), 0)),
            pl.BlockSpec((HALO, D_MODEL),
                         lambda i: (jnp.minimum((cur(i) + 1) * per_block, n_halo_blocks - 1), 0)),
            pl.BlockSpec((GROUP // V7X_LANES, 1, k_lo_blocks, 1, MIXER_SUB, tile_rows, V7X_LANES),
                         lambda i: (0, cur(i) // blocks_per_seq, 0, cur(i) % blocks_per_seq, 0, 0, 0)),
            _layer_spec(layer, p["gains"].shape[1:]),
            _layer_spec(layer, p["vec"].shape[1:]),
            _layer_spec(layer, p["mats"].shape[1:]),
            _layer_spec(layer, p["wcat"].shape[1:]),
            _layer_spec(layer, p["w_rows"].shape[1:]),
            _layer_spec(layer, p["wd"].shape[1:]),
        ],
        out_specs=pl.BlockSpec((block, D_MODEL), lambda i: (jnp.maximum(i - 1, 0), 0)),
        out_shape=jax.ShapeDtypeStruct(x2.shape, x2.dtype),
        scratch_shapes=[
            pltpu.VMEM((tile + 2 * HALO, GROUP), F32),
            pltpu.VMEM((tile + 2 * HALO + V7X_SUBLANES, GROUP), F32),
            pltpu.VMEM((2, tile + 2 * HALO, V7X_LANES), F32),
            pltpu.VMEM((tile, D_FF), BF16),
            pltpu.VMEM((MIXER_SUB, tile, D_MODEL), F32),
            pltpu.VMEM((MIXER_SUB, tile, D_MODEL), BF16),
            pltpu.VMEM((D_MODEL, GROUP), BF16),
        ],
        compiler_params=pltpu.CompilerParams(
            dimension_semantics=("arbitrary",),
            vmem_limit_bytes=V7X_VMEM_LIMIT_BYTES),
        name="mixer_ffn",
    )(x2, x2, x2, spec, p["gains"], p["vec"], p["mats"], p["wcat"], p["w_rows"], p["wd"])


def _block_diag(blocks):
    depth, n, k, _ = blocks.shape
    on_diagonal = jnp.eye(n, dtype=bool)[None, :, None, :, None]
    return jnp.where(on_diagonal, blocks[:, :, :, None, :], 0.0).reshape(depth, n * k, n * k)


def _pack_params(pre_mix_gain, post_mix_gain, pre_ffn_gain, post_ffn_gain, w_in, conv_w, pool_w, pool_scale,
                 fourier_w, spatial_w, spatial_b, group_norm_gain, w_out, w_gate, w_up, w_down):
    depth = w_in.shape[0]
    sl = lambda k: w_in[:, :, k * GROUP:(k + 1) * GROUP]
    w_rows = jnp.concatenate([sl(1), sl(2), sl(3), sl(0), sl(5), sl(6), sl(4), w_out, w_gate, w_up],
                             axis=2).astype(BF16)
    assert w_rows.shape[2] == W_ROWS_COLS
    mavg = np.kron(np.eye(N_HEADS), np.full((HEAD_DIM, HEAD_DIM), 1.0 / HEAD_DIM))
    mavg = jnp.broadcast_to(jnp.asarray(mavg, F32), (depth, GROUP, GROUP))
    sbias = jnp.repeat(jnp.transpose(spatial_b, (0, 2, 1)), HEAD_DIM, axis=2)
    row = lambda g: g[:, None, :]
    return dict(
        gains=jnp.concatenate([row(pre_mix_gain), row(post_mix_gain), row(pre_ffn_gain), row(post_ffn_gain),
                               row(group_norm_gain), jnp.zeros((depth, V7X_SUBLANES - 5, D_MODEL), F32)],
                              axis=1),
        vec=jnp.concatenate([sbias, conv_w, row(pool_scale),
                             jnp.zeros((depth, V7X_SUBLANES - 4, GROUP), F32)], axis=1),
        mats=jnp.stack([_block_diag(pool_w), _block_diag(fourier_w), mavg], axis=1).astype(BF16),
        wcat=jnp.transpose(spatial_w, (0, 2, 1, 3)).reshape(depth, CHUNK, N_HEADS * CHUNK).astype(BF16),
        w_rows=w_rows,
        wd=w_down.astype(BF16),
    )


def kernel(x, pre_mix_gain, post_mix_gain, pre_ffn_gain, post_ffn_gain, w_in, conv_w, pool_w, pool_scale,
           fourier_w, spatial_w, spatial_b, group_norm_gain, w_out, w_gate, w_up, w_down):
    b, seq, d = x.shape
    assert d == D_MODEL and seq % SEQ_LO == 0 and seq % (MIXER_SUB * min(TOKEN_TILE, seq)) == 0
    tables = _fourier_tables(seq)
    p = _pack_params(pre_mix_gain, post_mix_gain, pre_ffn_gain, post_ffn_gain, w_in, conv_w, pool_w,
                     pool_scale, fourier_w, spatial_w, spatial_b, group_norm_gain, w_out, w_gate, w_up,
                     w_down)
    for layer in range(w_in.shape[0]):
        spec = _fourier_spec(x, layer, p, tables)
        x = _mixer_ffn(x.reshape(b * seq, d), spec, seq, layer, p).reshape(b, seq, d)
    return x
```

```python
import functools

import jax
import jax.numpy as jnp
import numpy as np
from jax import lax
from jax.experimental import pallas as pl
from jax.experimental.pallas import tpu as pltpu

D_MODEL = 1024
GROUP = 256
HEAD_DIM = 64
N_HEADS = GROUP // HEAD_DIM
CHUNK = 128
D_FF = 2816
POOL_WINDOWS = (2, 4, 8, 16)
EPS = 1e-6

V7X_LANES = 128
V7X_SUBLANES = 8
V7X_MXU_COLS = 256
V7X_VMEM_LIMIT_BYTES = 56 * 1024 * 1024

SEQ_LO = 128
HALO = 16
TOKEN_TILE = 512
MIXER_SUB = 1
FF_CHUNK = V7X_MXU_COLS
W_IN_COL = 0
W_F_COL = W_IN_COL + 6 * GROUP
W_OUT_COL = W_F_COL + GROUP
W_GATE_COL = W_OUT_COL + D_MODEL
W_UP_COL = W_GATE_COL + D_FF
W_ROWS_COLS = W_UP_COL + D_FF
FOURIER_SUB = 4

F32 = jnp.float32
BF16 = jnp.bfloat16


def _dot(a, b):
    return jnp.dot(a, b, preferred_element_type=F32)


def _rms(x, g):
    ms = jnp.mean(x * x, axis=-1, keepdims=True)
    return x * lax.rsqrt(ms + EPS) * g


def _fourier_tables(seq):
    seq_hi = seq // SEQ_LO
    cm = np.outer(np.arange(HEAD_DIM), np.arange(HEAD_DIM)) % HEAD_DIM
    ang = 2.0 * np.pi * cm / HEAD_DIM
    cos_bd = np.kron(np.eye(N_HEADS), np.cos(ang)) / np.sqrt(HEAD_DIM)
    sin_bd = np.kron(np.eye(N_HEADS), np.sin(ang)) / np.sqrt(HEAD_DIM)
    fc = np.block([[cos_bd, -sin_bd], [sin_bd, cos_bd]])
    kt = np.outer(np.arange(seq_hi), np.arange(seq_hi)) % seq_hi
    ang1 = 2.0 * np.pi * kt / seq_hi
    eye = np.eye(V7X_SUBLANES)
    c1 = np.kron(np.cos(ang1) / np.sqrt(seq_hi), eye)
    s1 = np.kron(np.sin(ang1) / np.sqrt(seq_hi), eye)
    kd = np.concatenate([c1, -s1], axis=0)
    k = np.arange(SEQ_LO)[None, :, None] * seq_hi + np.arange(seq_hi)[:, None, None]
    ang2 = 2.0 * np.pi * ((k * np.arange(SEQ_LO)[None, None, :]) % seq) / seq
    e = np.concatenate([np.cos(ang2), np.sin(ang2)], axis=2) / np.sqrt(SEQ_LO)
    return (jnp.asarray(fc, F32).astype(BF16), jnp.asarray(kd, F32).astype(BF16),
            jnp.asarray(e, F32).astype(BF16))


def _fourier_kernel(x_ref, g_ref, wf_ref, fc_ref, kd_ref, e_ref, o_ref, y_s, f_s, *, seq_hi, n_tb):
    step = pl.program_id(0)
    tb = jnp.maximum(step - 1, 0) % n_tb
    rows = seq_hi * V7X_SUBLANES
    t_lo_per_step = V7X_SUBLANES * FOURIER_SUB

    @pl.when(step == 0)
    def _():
        f_s[...] = jnp.zeros_like(f_s)

    def stage_norm(q):
        x = x_ref[0, :, q * V7X_SUBLANES:(q + 1) * V7X_SUBLANES, :].reshape(rows, D_MODEL)
        inv = lax.rsqrt(jnp.mean(x * x, axis=-1, keepdims=True) + EPS)
        return (x * g_ref[0:1, :]).astype(BF16), inv

    def stage_proj(q, h_inv):
        h, inv = h_inv
        return (_dot(h, wf_ref[...]) * inv).astype(BF16)

    def stage_slow_dft(q, f):
        u = _dot(kd_ref[...], f)
        return jnp.concatenate([u[:rows], u[rows:]], axis=1).astype(BF16)

    def stage_channel_dft(q, uc):
        return _dot(uc, fc_ref[...])

    vals = []
    for q in range(FOURIER_SUB):
        h_inv = stage_norm(q)
        uc = stage_slow_dft(q, f_s[q])
        f_s[q] = stage_proj(q, h_inv)
        vals.append(stage_channel_dft(q, uc))

    row = pl.multiple_of(tb * t_lo_per_step, t_lo_per_step)
    for k in range(seq_hi):
        rs = slice(k * V7X_SUBLANES, (k + 1) * V7X_SUBLANES)
        ls = slice(k * GROUP, (k + 1) * GROUP)
        piece = jnp.concatenate([v[rs] for v in vals], axis=0).astype(BF16)
        y_s[pl.ds(row, t_lo_per_step), ls] = piece[:, :GROUP]
        y_s[pl.ds(SEQ_LO + row, t_lo_per_step), ls] = piece[:, GROUP:]

    @pl.when((step > 0) & (tb == n_tb - 1))
    def _():
        def out_block(kb, carry):
            for j in range(V7X_SUBLANES):
                k_lo = kb * V7X_SUBLANES + j
                res = _dot(e_ref[k_lo], y_s[:, pl.ds(pl.multiple_of(k_lo * GROUP, GROUP), GROUP)])
                for half in range(GROUP // V7X_LANES):
                    ls = slice(half * V7X_LANES, (half + 1) * V7X_LANES)
                    o_ref[half, 0, kb, pl.ds(j, SEQ_LO, stride=V7X_SUBLANES), :] = res[:, ls]
            return carry

        lax.fori_loop(0, seq_hi // V7X_SUBLANES, out_block, 0, unroll=True)


def _const_spec(shape):
    n = len(shape)
    return pl.BlockSpec(shape, lambda *_: (0,) * n, pipeline_mode=pl.Buffered(1))


def _layer_spec(layer, shape, block_index=None):
    index = (0,) * len(shape) if block_index is None else block_index
    return pl.BlockSpec((None,) + shape, lambda *_: (layer,) + index, pipeline_mode=pl.Buffered(1))


def _fourier_spec(x, layer, params, tables):
    b, seq, _ = x.shape
    seq_hi = seq // SEQ_LO
    t_lo_per_step = V7X_SUBLANES * FOURIER_SUB
    n_tb = SEQ_LO // t_lo_per_step
    rows = seq_hi * V7X_SUBLANES
    fc, kd, e = tables
    halves = GROUP // V7X_LANES
    k_lo_blocks = seq_hi // V7X_SUBLANES
    out_block = (halves, 1, k_lo_blocks, SEQ_LO * V7X_SUBLANES, V7X_LANES)
    x4 = x.reshape(b, seq_hi, SEQ_LO, D_MODEL)
    n_blocks = b * n_tb
    proj_block = lambda s: jnp.minimum(s, n_blocks - 1)
    dft_batch = lambda s: jnp.maximum(s - 1, 0) // n_tb
    out = pl.pallas_call(
        functools.partial(_fourier_kernel, seq_hi=seq_hi, n_tb=n_tb),
        grid=(n_blocks + 1,),
        in_specs=[
            pl.BlockSpec((1, seq_hi, t_lo_per_step, D_MODEL),
                         lambda s: (proj_block(s) // n_tb, 0, proj_block(s) % n_tb, 0)),
            _layer_spec(layer, (V7X_SUBLANES, D_MODEL)),
            _layer_spec(layer, (D_MODEL, GROUP), (0, W_F_COL // GROUP)),
            _const_spec((2 * GROUP, 2 * GROUP)),
            _const_spec((2 * rows, rows)),
            _const_spec((seq_hi, SEQ_LO, 2 * SEQ_LO)),
        ],
        out_specs=pl.BlockSpec(out_block, lambda s: (0, dft_batch(s), 0, 0, 0)),
        out_shape=jax.ShapeDtypeStruct((halves, b) + out_block[2:], F32),
        scratch_shapes=[
            pltpu.VMEM((2 * SEQ_LO, seq_hi * GROUP), BF16),
            pltpu.VMEM((FOURIER_SUB, rows, GROUP), BF16),
        ],
        compiler_params=pltpu.CompilerParams(
            dimension_semantics=("arbitrary",),
            vmem_limit_bytes=V7X_VMEM_LIMIT_BYTES),
        name="fourier_spec",
    )(x4, params["gains"], params["w_rows"], fc, kd, e)
    return out


def _mixer_ffn_kernel(x_ref, xp_ref, xn_ref, spec_ref, gains_ref, vec_ref, mats_ref, wcat_ref, w_rows_ref,
                      wd_ref,
                      o_ref, cz_s, p_s, ps_s, act_s, x1_all_s, h2_all_s, *, tile, seq, n_blocks):
    g_pre_ref, g_post_ref, g_pre_ffn_ref, g_post_ffn_ref, g_grp_ref = (
        gains_ref.at[k:k + 1] for k in range(5))
    sbias_ref = vec_ref.at[0:CHUNK]
    conv_w_ref = vec_ref.at[CHUNK:CHUNK + 3]
    pool_scale_ref = vec_ref.at[CHUNK + 3:CHUNK + 4]
    pool_w_ref, four_w_ref, mavg_ref = (mats_ref.at[k] for k in range(3))
    w_in_ref = w_rows_ref.at[:, W_IN_COL:W_IN_COL + 6 * GROUP]
    w_out_ref = w_rows_ref.at[:, W_OUT_COL:W_OUT_COL + D_MODEL]
    wg_ref = w_rows_ref.at[:, W_GATE_COL:W_GATE_COL + D_FF]
    wu_ref = w_rows_ref.at[:, W_UP_COL:W_UP_COL + D_FF]
    step = pl.program_id(0)

    @pl.when(step == 0)
    def _():
        x1_all_s[...] = jnp.zeros_like(x1_all_s)
        h2_all_s[...] = jnp.zeros_like(h2_all_s)

    blk = jnp.minimum(step, n_blocks - 1)
    tiles_per_seq = seq // tile
    for k in range(MIXER_SUB):
        rows_k = pl.ds(k * tile, tile)
        prev_ref = xp_ref if k == 0 else x_ref.at[pl.ds(k * tile - HALO, HALO)]
        next_ref = xn_ref if k == MIXER_SUB - 1 else x_ref.at[pl.ds((k + 1) * tile, HALO)]
        _mixer_ffn_tile(
            x_ref.at[rows_k], prev_ref, next_ref, spec_ref.at[:, 0, :, 0, k], o_ref.at[rows_k],
            x1_all_s.at[k], h2_all_s.at[k], (blk * MIXER_SUB + k) % tiles_per_seq, tiles_per_seq,
            g_pre_ref, g_post_ref, g_pre_ffn_ref, g_post_ffn_ref, g_grp_ref, sbias_ref, conv_w_ref,
            pool_scale_ref, pool_w_ref, four_w_ref, mavg_ref, wcat_ref, w_in_ref, w_out_ref, wg_ref, wu_ref,
            wd_ref, cz_s, p_s, ps_s, act_s, tile=tile, seq=seq)


def _mixer_ffn_tile(x_ref, xp_ref, xn_ref, spec_ref, o_ref, x1_s, h2_s, ti, tiles_per_seq,
                    g_pre_ref, g_post_ref, g_pre_ffn_ref, g_post_ffn_ref, g_grp_ref, sbias_ref, conv_w_ref,
                    pool_scale_ref, pool_w_ref, four_w_ref, mavg_ref, wcat_ref, w_in_ref, w_out_ref, wg_ref,
                    wu_ref, wd_ref, cz_s, p_s, ps_s, act_s, *, tile, seq):
    def ffn_chunks(lo, hi):
        for j in range(lo, hi):
            ls = slice(j * FF_CHUNK, (j + 1) * FF_CHUNK)
            gate = _dot(h2_s[...], wg_ref[:, ls])
            up = _dot(h2_s[...], wu_ref[:, ls])
            act_s[:, ls] = (gate * jax.nn.sigmoid(gate) * up).astype(BF16)

    def ffn_down(rows):
        f = _dot(act_s[rows, :], wd_ref[...])
        o_ref[rows, :] = x1_s[rows, :] + _rms(f, g_post_ffn_ref[...])

    ffn_chunks(0, 1)

    x = x_ref[...]
    xh = jnp.concatenate([xp_ref[...], xn_ref[...]], axis=0)

    hm = _rms(x, g_pre_ref[...]).astype(BF16)
    hh = _rms(xh, g_pre_ref[...]).astype(BF16)
    ffn_chunks(1, 2)
    pe = _dot(jnp.concatenate([hm, hh], axis=0), w_in_ref[:, :3 * GROUP])
    pm = jnp.concatenate([pe[:tile], _dot(hm, w_in_ref[:, 3 * GROUP:])], axis=1)
    ph = pe[tile:]
    ffn_chunks(2, 3)
    hrow = lax.broadcasted_iota(jnp.int32, (2 * HALO, 1), 0)
    has_prev = (ti != 0).astype(jnp.int32)
    has_next = (ti != tiles_per_seq - 1).astype(jnp.int32)
    ph = jnp.where(jnp.where(hrow < HALO, has_prev, has_next) != 0, ph, 0.0)

    cz_s[pl.ds(0, HALO), :] = ph[:HALO, :GROUP] * ph[:HALO, GROUP:2 * GROUP]
    cz_s[pl.ds(HALO, tile), :] = pm[:, :GROUP] * pm[:, GROUP:2 * GROUP]
    cz_s[pl.ds(HALO + tile, HALO), :] = ph[HALO:, :GROUP] * ph[HALO:, GROUP:2 * GROUP]
    p_s[pl.ds(0, HALO), :] = ph[:HALO, 2 * GROUP:]
    p_s[pl.ds(HALO, tile), :] = pm[:, 2 * GROUP:3 * GROUP]
    p_s[pl.ds(HALO + tile, HALO), :] = ph[HALO:, 2 * GROUP:]
    p_s[pl.ds(2 * HALO + tile, V7X_SUBLANES), :] = jnp.zeros((V7X_SUBLANES, GROUP), F32)
    b_gate = pm[:, 3 * GROUP:4 * GROUP]
    u = pm[:, 4 * GROUP:5 * GROUP]
    v = pm[:, 5 * GROUP:]

    y_conv = b_gate * (conv_w_ref[0:1, :] * cz_s[pl.ds(HALO - 1, tile), :]
                       + conv_w_ref[1:2, :] * cz_s[pl.ds(HALO, tile), :]
                       + conv_w_ref[2:3, :] * cz_s[pl.ds(HALO + 1, tile), :])
    ffn_chunks(3, 4)

    pos = ti * tile + lax.broadcasted_iota(jnp.int32, (tile, 1), 0)
    lane = lax.broadcasted_iota(jnp.int32, (1, V7X_LANES), 1)
    first_half = lane < HEAD_DIM

    def pool_col(col, inner, outer):
        w_small, w_big = POOL_WINDOWS[2 * col], POOL_WINDOWS[2 * col + 1]
        half = jnp.where(first_half, w_small // 2, w_big // 2)
        cnt = (jnp.minimum(pos + half, seq) - jnp.maximum(pos - half, 0)).astype(F32)
        mean = jnp.where(first_half, inner, outer) / cnt
        return mean - p_s[pl.ds(HALO, tile), col * V7X_LANES:(col + 1) * V7X_LANES]

    ls0 = slice(0, V7X_LANES)
    inner0 = p_s[pl.ds(HALO - 1, tile), ls0] + p_s[pl.ds(HALO, tile), ls0]
    outer0 = inner0 + p_s[pl.ds(HALO - 2, tile), ls0] + p_s[pl.ds(HALO + 1, tile), ls0]
    d_col0 = pool_col(0, inner0, outer0)
    ffn_chunks(4, 5)
    ls1 = slice(V7X_LANES, 2 * V7X_LANES)
    base = HALO - POOL_WINDOWS[3] // 2
    n2, n4, n8 = tile + 24, tile + 16, tile + 8
    ps_s[0, pl.ds(base, n2), :] = p_s[pl.ds(base, n2), ls1] + p_s[pl.ds(base + 1, n2), ls1]
    ps_s[1, pl.ds(base, n4), :] = ps_s[0, pl.ds(base, n4), :] + ps_s[0, pl.ds(base + 2, n4), :]
    ps_s[0, pl.ds(base, n8), :] = ps_s[1, pl.ds(base, n8), :] + ps_s[1, pl.ds(base + 4, n8), :]
    inner1 = ps_s[0, pl.ds(HALO - 4, tile), :]
    outer1 = ps_s[0, pl.ds(HALO - 8, tile), :] + ps_s[0, pl.ds(HALO, tile), :]
    d = jnp.concatenate([d_col0, pool_col(1, inner1, outer1)], axis=1).astype(BF16)
    y_pool = _dot(d, pool_w_ref[...]) * pool_scale_ref[...]

    n_kb, n_kh = spec_ref.shape[1], spec_ref.shape[2] // V7X_SUBLANES
    spec = jnp.concatenate(
        [jnp.concatenate([spec_ref[half, kb, pl.ds(i * V7X_SUBLANES, V7X_SUBLANES), :]
                          for i in range(n_kh) for kb in range(n_kb)], axis=0)
         for half in range(GROUP // V7X_LANES)], axis=1).astype(BF16)
    y_four = _dot(spec, four_w_ref[...])
    ffn_chunks(5, 6)

    mu = _dot(v.astype(BF16), mavg_ref[...])
    vc = v - mu
    var = _dot((vc * vc).astype(BF16), mavg_ref[...])
    vh = vc * lax.rsqrt(var + EPS)
    ffn_chunks(6, 7)
    lane_g = lax.broadcasted_iota(jnp.int32, (1, GROUP), 1)
    s_chunks = []
    for n in range(tile // CHUNK):
        vn = vh[n * CHUNK:(n + 1) * CHUNK]
        stack = jnp.concatenate(
            [jnp.where(lane_g // HEAD_DIM == hd, vn, 0.0).astype(BF16) for hd in range(N_HEADS)], axis=0)
        s_chunks.append(_dot(wcat_ref[...], stack) + sbias_ref[...])
    y_gmlp = u * jnp.concatenate(s_chunks, axis=0)
    ffn_chunks(7, 8)

    y = jnp.concatenate(
        [_rms(yk, g_grp_ref[:, k * GROUP:(k + 1) * GROUP]).astype(BF16)
         for k, yk in enumerate((y_conv, y_pool, y_four, y_gmlp))], axis=1)
    ffn_chunks(8, 9)

    def out_half(rows):
        m = _dot(y[rows, :], w_out_ref[...])
        x1 = x_ref[rows, :] + _rms(m, g_post_ref[...])
        return x1, _rms(x1, g_pre_ffn_ref[...]).astype(BF16)

    top, bot = slice(0, tile // 2), slice(tile // 2, tile)
    x1_top, h2_top = out_half(top)
    ffn_chunks(9, 10)
    x1_bot, h2_bot = out_half(bot)
    ffn_chunks(10, D_FF // FF_CHUNK)
    ffn_down(top)
    x1_s[top, :] = x1_top
    h2_s[top, :] = h2_top
    ffn_down(bot)
    x1_s[bot, :] = x1_bot
    h2_s[bot, :] = h2_bot


def _mixer_ffn(x2, spec, seq, layer, p):
    n_tok = x2.shape[0]
    tile = min(TOKEN_TILE, seq)
    block = MIXER_SUB * tile
    n_blocks = n_tok // block
    n_halo_blocks = n_tok // HALO
    per_block = block // HALO
    seq_hi = seq // SEQ_LO
    k_lo_blocks = seq_hi // V7X_SUBLANES
    tile_rows = tile // seq_hi * V7X_SUBLANES
    blocks_per_seq = seq // block
    spec = spec.reshape(spec.shape[:3] + (blocks_per_seq, MIXER_SUB, tile_rows, V7X_LANES))
    cur = lambda i: jnp.minimum(i, n_blocks - 1)
    return pl.pallas_call(
        functools.partial(_mixer_ffn_kernel, tile=tile, seq=seq, n_blocks=n_blocks),
        grid=(n_blocks + 1,),
        in_specs=[
            pl.BlockSpec((block, D_MODEL), lambda i: (cur(i), 0)),
            pl.BlockSpec((HALO, D_MODEL), lambda i: (jnp.maximum(cur(i) * per_block - 1, 0), 0)),
            pl.BlockSpec((HALO, D_MODEL),
                         lambda i: (jnp.minimum((cur(i) + 1) * per_block, n_halo_blocks - 1), 0)),
            pl.BlockSpec((GROUP // V7X_LANES, 1, k_lo_blocks, 1, MIXER_SUB, tile_rows, V7X_LANES),
                         lambda i: (0, cur(i) // blocks_per_seq, 0, cur(i) % blocks_per_seq, 0, 0, 0)),
            _layer_spec(layer, p["gains"].shape[1:]),
            _layer_spec(layer, p["vec"].shape[1:]),
            _layer_spec(layer, p["mats"].shape[1:]),
            _layer_spec(layer, p["wcat"].shape[1:]),
            _layer_spec(layer, p["w_rows"].shape[1:]),
            _layer_spec(layer, p["wd"].shape[1:]),
        ],
        out_specs=pl.BlockSpec((block, D_MODEL), lambda i: (jnp.maximum(i - 1, 0), 0)),
        out_shape=jax.ShapeDtypeStruct(x2.shape, x2.dtype),
        scratch_shapes=[
            pltpu.VMEM((tile + 2 * HALO, GROUP), F32),
            pltpu.VMEM((tile + 2 * HALO + V7X_SUBLANES, GROUP), F32),
            pltpu.VMEM((2, tile + 2 * HALO, V7X_LANES), F32),
            pltpu.VMEM((tile, D_FF), BF16),
            pltpu.VMEM((MIXER_SUB, tile, D_MODEL), F32),
            pltpu.VMEM((MIXER_SUB, tile, D_MODEL), BF16),
        ],
        compiler_params=pltpu.CompilerParams(
            dimension_semantics=("arbitrary",),
            vmem_limit_bytes=V7X_VMEM_LIMIT_BYTES),
        name="mixer_ffn",
    )(x2, x2, x2, spec, p["gains"], p["vec"], p["mats"], p["wcat"], p["w_rows"], p["wd"])


def _block_diag(blocks):
    depth, n, k, _ = blocks.shape
    on_diagonal = jnp.eye(n, dtype=bool)[None, :, None, :, None]
    return jnp.where(on_diagonal, blocks[:, :, :, None, :], 0.0).reshape(depth, n * k, n * k)


def _pack_params(pre_mix_gain, post_mix_gain, pre_ffn_gain, post_ffn_gain, w_in, conv_w, pool_w, pool_scale,
                 fourier_w, spatial_w, spatial_b, group_norm_gain, w_out, w_gate, w_up, w_down):
    depth = w_in.shape[0]
    sl = lambda k: w_in[:, :, k * GROUP:(k + 1) * GROUP]
    w_rows = jnp.concatenate([sl(1), sl(2), sl(3), sl(0), sl(5), sl(6), sl(4), w_out, w_gate, w_up],
                             axis=2).astype(BF16)
    assert w_rows.shape[2] == W_ROWS_COLS
    mavg = np.kron(np.eye(N_HEADS), np.full((HEAD_DIM, HEAD_DIM), 1.0 / HEAD_DIM))
    mavg = jnp.broadcast_to(jnp.asarray(mavg, F32), (depth, GROUP, GROUP))
    sbias = jnp.repeat(jnp.transpose(spatial_b, (0, 2, 1)), HEAD_DIM, axis=2)
    row = lambda g: g[:, None, :]
    return dict(
        gains=jnp.concatenate([row(pre_mix_gain), row(post_mix_gain), row(pre_ffn_gain), row(post_ffn_gain),
                               row(group_norm_gain), jnp.zeros((depth, V7X_SUBLANES - 5, D_MODEL), F32)],
                              axis=1),
        vec=jnp.concatenate([sbias, conv_w, row(pool_scale),
                             jnp.zeros((depth, V7X_SUBLANES - 4, GROUP), F32)], axis=1),
        mats=jnp.stack([_block_diag(pool_w), _block_diag(fourier_w), mavg], axis=1).astype(BF16),
        wcat=jnp.transpose(spatial_w, (0, 2, 1, 3)).reshape(depth, CHUNK, N_HEADS * CHUNK).astype(BF16),
        w_rows=w_rows,
        wd=w_down.astype(BF16),
    )


def kernel(x, pre_mix_gain, post_mix_gain, pre_ffn_gain, post_ffn_gain, w_in, conv_w, pool_w, pool_scale,
           fourier_w, spatial_w, spatial_b, group_norm_gain, w_out, w_gate, w_up, w_down):
    b, seq, d = x.shape
    assert d == D_MODEL and seq % SEQ_LO == 0 and seq % (MIXER_SUB * min(TOKEN_TILE, seq)) == 0
    tables = _fourier_tables(seq)
    p = _pack_params(pre_mix_gain, post_mix_gain, pre_ffn_gain, post_ffn_gain, w_in, conv_w, pool_w,
                     pool_scale, fourier_w, spatial_w, spatial_b, group_norm_gain, w_out, w_gate, w_up,
                     w_down)
    for layer in range(w_in.shape[0]):
        spec = _fourier_spec(x, layer, p, tables)
        x = _mixer_ffn(x.reshape(b * seq, d), spec, seq, layer, p).reshape(b, seq, d)
    return x
```
